```python
import math
import jax, jax.numpy as jnp
from jax import lax
import numpy as np

D_MODEL = 2048
BATCH = 2
SEQ = 8192
DEPTH = 1
DEC_BATCH = 32
DEC_SEQ = 16
PAST_LEN = 4096

CHUNK = 64
N_HEADS = 8
HEAD_DIM = 128
ROT_DIM = HEAD_DIM // 4
ROPE_THETA = 500000.0
Q_BLOCK = 128
ATT_W = N_HEADS * 2 * HEAD_DIM
CONV_DIM = 2048
CONV_W = 31
PEER_HEADS = 8
PEER_NKEYS = 128
PEER_EXPERTS = PEER_NKEYS * PEER_NKEYS
PEER_QDIM = 256
PEER_TOPK = 16
PEER_BLOCK = 128
PLE_DIM = 256
IN_COLS = 3 * ATT_W + 2 * CONV_DIM + 2 * D_MODEL
DN_ALPHA = (2 * DEPTH) ** 0.25
DN_BETA = (8 * DEPTH) ** -0.25
LN_EPS = 1e-5

kernel_name = "diffattn_conformer_peer_streaming_encoder"

F32 = jnp.float32


def layer_norm(x, g, b):
    xf = x.astype(F32)
    mu = jnp.mean(xf, -1, keepdims=True)
    var = jnp.mean(jnp.square(xf - mu), -1, keepdims=True)
    return ((xf - mu) * lax.rsqrt(var + LN_EPS) * g + b).astype(x.dtype)


def rms_norm(x, g):
    xf = x.astype(F32)
    return (xf * lax.rsqrt(jnp.mean(xf * xf, -1, keepdims=True) + LN_EPS) * g).astype(x.dtype)


def rope(x, pos):
    inv = ROPE_THETA ** (-jnp.arange(0, ROT_DIM, 2, dtype=F32) / ROT_DIM)
    ang = pos.astype(F32)[:, None] * inv[None, :]
    cos = jnp.cos(ang)[:, None, None, :].astype(x.dtype)
    sin = jnp.sin(ang)[:, None, None, :].astype(x.dtype)
    half = ROT_DIM // 2
    x1 = x[..., :half]
    x2 = x[..., half:ROT_DIM]
    return jnp.concatenate([x1 * cos - x2 * sin, x2 * cos + x1 * sin, x[..., ROT_DIM:]], -1)


def diff_mix(scores, v, lam):
    p = jax.nn.softmax(scores, axis=-1)
    a = p[:, :, 0] - lam * p[:, :, 1]
    return jnp.einsum('bhqk,bkhe->bqhe', a.astype(v.dtype), v)


def attn_prompt(q, k, v, lam):
    b, s = q.shape[:2]
    nb = s // Q_BLOCK
    scale = HEAD_DIM ** -0.5
    qb = q.reshape(b, nb, Q_BLOCK, N_HEADS, 2, HEAD_DIM).swapaxes(0, 1)
    kchunk = jnp.arange(s) // CHUNK

    def one(args):
        qi, i = args
        sc = jnp.einsum('bqhmd,bkhmd->bhmqk', qi, k, preferred_element_type=F32) * scale
        qchunk = (i * Q_BLOCK + jnp.arange(Q_BLOCK)) // CHUNK
        mask = kchunk[None, :] <= qchunk[:, None]
        sc = jnp.where(mask, sc, -jnp.inf)
        return diff_mix(sc, v, lam)

    out = lax.map(one, (qb, jnp.arange(nb)))
    return out.swapaxes(0, 1).reshape(b, s, N_HEADS, 2 * HEAD_DIM)


def attn_sample(q, k_all, v_all, lam):
    sc = jnp.einsum('bqhmd,bkhmd->bhmqk', q, k_all, preferred_element_type=F32) * (HEAD_DIM ** -0.5)
    return diff_mix(sc, v_all, lam)


def conv_branch(u_glu, conv_past, w_dw, b_dw, cln_g, cln_b, w_conv_out):
    a, gt = jnp.split(u_glu, 2, axis=-1)
    c = a * jax.nn.sigmoid(gt)
    cpad = jnp.concatenate([conv_past, c], axis=1)
    y = lax.conv_general_dilated(cpad, w_dw[:, None, :], (1,), 'VALID',
                                 dimension_numbers=('NWC', 'WIO', 'NWC'),
                                 feature_group_count=CONV_DIM) + b_dw
    y = jax.nn.silu(layer_norm(y, cln_g, cln_b))
    return y @ w_conv_out, cpad[:, -(CONV_W - 1):]


def peer(x, peer_wq, peer_keys, peer_u, peer_v):
    n = x.shape[0]
    pad = (-n) % PEER_BLOCK
    xb = jnp.pad(x, ((0, pad), (0, 0))).reshape(-1, PEER_BLOCK, D_MODEL)

    def one(xi):
        q = (xi @ peer_wq).reshape(PEER_BLOCK, PEER_HEADS, 2, PEER_QDIM // 2)
        s = jnp.einsum('nhcd,hckd->nhck', q, peer_keys, preferred_element_type=F32)
        v1, i1 = lax.top_k(s[:, :, 0], PEER_TOPK)
        v2, i2 = lax.top_k(s[:, :, 1], PEER_TOPK)
        cand = (v1[..., :, None] + v2[..., None, :]).reshape(PEER_BLOCK, PEER_HEADS, PEER_TOPK * PEER_TOPK)
        cidx = (i1[..., :, None] * PEER_NKEYS + i2[..., None, :]).reshape(PEER_BLOCK, PEER_HEADS, PEER_TOPK * PEER_TOPK)
        sv, sj = lax.top_k(cand, PEER_TOPK)
        eidx = jnp.take_along_axis(cidx, sj, axis=-1)
        g = jax.nn.softmax(sv, axis=-1)
        h = jnp.einsum('nd,nhkd->nhk', xi, peer_u[eidx], preferred_element_type=F32)
        act = (jax.nn.gelu(h) * g).astype(xi.dtype)
        return jnp.einsum('nhk,nhkd->nd', act, peer_v[eidx])

    return lax.map(one, xb).reshape(-1, D_MODEL)[:n]


def layer(x, pe, pos, k_past, v_past, conv_past, layer_idx, w_in, lam_q, lam_k, subln_g,
          w_att_out, w_dw, b_dw, cln_g, cln_b, w_conv_out, w_o, ln1_g, ln1_b,
          peer_wq, peer_keys, peer_u, peer_v, ln2_g, ln2_b, w_ple, w_ple_gate):
    b, t, _ = x.shape
    proj = x @ w_in
    q, k, v, u_glu, gates = jnp.split(
        proj, [ATT_W, 2 * ATT_W, 3 * ATT_W, 3 * ATT_W + 2 * CONV_DIM], axis=-1)
    q = rope(q.reshape(b, t, N_HEADS, 2, HEAD_DIM), pos)
    k = rope(k.reshape(b, t, N_HEADS, 2, HEAD_DIM), pos)
    v = v.reshape(b, t, N_HEADS, 2 * HEAD_DIM)

    lam_init = 0.8 - 0.6 * math.exp(-0.3 * layer_idx)
    lf = lam_q.astype(F32) * lam_k.astype(F32)
    lam = jnp.exp(jnp.sum(lf[0])) - jnp.exp(jnp.sum(lf[1])) + lam_init

    if k_past is None:
        o = attn_prompt(q, k, v, lam)
    else:
        kp = k_past.reshape(b, k_past.shape[1], N_HEADS, 2, HEAD_DIM)
        k_all = jnp.concatenate([kp, k], axis=1)
        v_all = jnp.concatenate([v_past, v], axis=1)
        o = attn_sample(q, k_all, v_all, lam)
    o = rms_norm(o, subln_g) * (1.0 - lam_init)
    attn_out = o.reshape(b, t, ATT_W) @ w_att_out

    conv_out, conv_state = conv_branch(u_glu, conv_past, w_dw, b_dw, cln_g, cln_b, w_conv_out)

    g_att, g_conv = jnp.split(jax.nn.sigmoid(gates), 2, axis=-1)
    mix = (g_att * attn_out + g_conv * conv_out) @ w_o
    x = layer_norm(DN_ALPHA * x + mix, ln1_g, ln1_b)

    ffn = peer(x.reshape(b * t, D_MODEL), peer_wq, peer_keys, peer_u, peer_v).reshape(b, t, D_MODEL)
    x = layer_norm(DN_ALPHA * x + ffn, ln2_g, ln2_b)

    x = x + jax.nn.sigmoid(x @ w_ple_gate) * (pe @ w_ple)
    return x, k.reshape(b, t, N_HEADS, 2 * HEAD_DIM), v, conv_state


def setup_inputs(seed: int = 0) -> dict:
    key = jax.random.key(seed)
    ks = jax.random.split(key, 32)
    nrm = lambda k, shape, s: jax.random.normal(k, shape, F32) * s
    return {
        "x_prompt": nrm(ks[0], (BATCH, SEQ, D_MODEL), 1.0),
        "x_sample": nrm(ks[1], (DEC_BATCH, DEC_SEQ, D_MODEL), 1.0),
        "cache_k": nrm(ks[2], (DEPTH, DEC_BATCH, PAST_LEN, N_HEADS, 2 * HEAD_DIM), 1.0),
        "cache_v": nrm(ks[3], (DEPTH, DEC_BATCH, PAST_LEN, N_HEADS, 2 * HEAD_DIM), 1.0),
        "state_conv": nrm(ks[4], (DEPTH, DEC_BATCH, CONV_W - 1, CONV_DIM), 0.5),
        "p_prompt": nrm(ks[5], (DEPTH, BATCH, SEQ, PLE_DIM), 1.0),
        "p_sample": nrm(ks[6], (DEPTH, DEC_BATCH, DEC_SEQ, PLE_DIM), 1.0),
        "w_in": nrm(ks[7], (DEPTH, D_MODEL, IN_COLS), D_MODEL ** -0.5),
        "lam_q": nrm(ks[8], (DEPTH, 2, HEAD_DIM), 0.1),
        "lam_k": nrm(ks[9], (DEPTH, 2, HEAD_DIM), 0.1),
        "subln_g": 1.0 + nrm(ks[10], (DEPTH, 2 * HEAD_DIM), 0.02),
        "w_att_out": nrm(ks[11], (DEPTH, ATT_W, D_MODEL), ATT_W ** -0.5),
        "w_dw": nrm(ks[12], (DEPTH, CONV_W, CONV_DIM), CONV_W ** -0.5),
        "b_dw": nrm(ks[13], (DEPTH, CONV_DIM), 0.02),
        "cln_g": 1.0 + nrm(ks[14], (DEPTH, CONV_DIM), 0.02),
        "cln_b": nrm(ks[15], (DEPTH, CONV_DIM), 0.02),
        "w_conv_out": nrm(ks[16], (DEPTH, CONV_DIM, D_MODEL), CONV_DIM ** -0.5),
        "w_o": nrm(ks[17], (DEPTH, D_MODEL, D_MODEL), DN_BETA * D_MODEL ** -0.5),
        "ln1_g": 1.0 + nrm(ks[18], (DEPTH, D_MODEL), 0.02),
        "ln1_b": nrm(ks[19], (DEPTH, D_MODEL), 0.02),
        "peer_wq": nrm(ks[20], (DEPTH, D_MODEL, PEER_HEADS * PEER_QDIM), D_MODEL ** -0.5),
        "peer_keys": nrm(ks[21], (DEPTH, PEER_HEADS, 2, PEER_NKEYS, PEER_QDIM // 2), (PEER_QDIM // 2) ** -0.5),
        "peer_u": nrm(ks[22], (DEPTH, PEER_EXPERTS, D_MODEL), D_MODEL ** -0.5),
        "peer_v": nrm(ks[23], (DEPTH, PEER_EXPERTS, D_MODEL), DN_BETA * PEER_HEADS ** -0.5),
        "ln2_g": 1.0 + nrm(ks[24], (DEPTH, D_MODEL), 0.02),
        "ln2_b": nrm(ks[25], (DEPTH, D_MODEL), 0.02),
        "w_ple": nrm(ks[26], (DEPTH, PLE_DIM, D_MODEL), PLE_DIM ** -0.5),
        "w_ple_gate": nrm(ks[27], (DEPTH, D_MODEL, D_MODEL), D_MODEL ** -0.5),
    }


def reference(x_prompt, x_sample, cache_k, cache_v, state_conv, p_prompt, p_sample,
              w_in, lam_q, lam_k, subln_g, w_att_out, w_dw, b_dw, cln_g, cln_b, w_conv_out,
              w_o, ln1_g, ln1_b, peer_wq, peer_keys, peer_u, peer_v, ln2_g, ln2_b,
              w_ple, w_ple_gate):
    xp = x_prompt
    xs = x_sample
    pos_p = jnp.arange(x_prompt.shape[1])
    pos_s = cache_k.shape[2] + jnp.arange(x_sample.shape[1])
    kp_l, vp_l, cp_l, ks_l, vs_l, cs_l = [], [], [], [], [], []
    for l in range(DEPTH):
        wts = (w_in[l], lam_q[l], lam_k[l], subln_g[l], w_att_out[l], w_dw[l], b_dw[l],
               cln_g[l], cln_b[l], w_conv_out[l], w_o[l], ln1_g[l], ln1_b[l],
               peer_wq[l], peer_keys[l], peer_u[l], peer_v[l], ln2_g[l], ln2_b[l],
               w_ple[l], w_ple_gate[l])
        zero_conv = jnp.zeros((xp.shape[0], CONV_W - 1, CONV_DIM), xp.dtype)
        xp, kp, vp, cp = layer(xp, p_prompt[l], pos_p, None, None, zero_conv, l, *wts)
        xs, ks_, vs_, cs_ = layer(xs, p_sample[l], pos_s, cache_k[l], cache_v[l], state_conv[l], l, *wts)
        kp_l.append(kp); vp_l.append(vp); cp_l.append(cp)
        ks_l.append(ks_); vs_l.append(vs_); cs_l.append(cs_)
    k_prompt = jnp.stack(kp_l)
    v_prompt = jnp.stack(vp_l)
    conv_prompt = jnp.stack(cp_l)
    k_sample = jnp.stack(ks_l)
    v_sample = jnp.stack(vs_l)
    conv_sample = jnp.stack(cs_l)
    return (xp, xs, k_prompt, v_prompt, conv_prompt, k_sample, v_sample, conv_sample)
```

```python
import functools
import math

import jax
import jax.numpy as jnp
from jax import lax
from jax.experimental import pallas as pl
from jax.experimental.pallas import tpu as pltpu

F32 = jnp.float32
BF16 = jnp.bfloat16

CHUNK = 64
N_HEADS = 8
HEAD_DIM = 128
HEAD_W = 2 * HEAD_DIM
ROT_DIM = HEAD_DIM // 4
ROPE_THETA = 500000.0
CONV_W = 31
CONV_HALO = 32
PEER_HEADS = 8
PEER_NKEYS = 128
PEER_TOPK = 16
LN_EPS = 1e-5
LOG2E = math.log2(math.e)
NEG_BIG = -1e30
LANES = 128
VMEM_LIMIT = 56 * 2**20


def _pick(n, prefs):
    for p in prefs:
        if n % p == 0:
            return p
    return n


def _params(*sem):
    return pltpu.CompilerParams(dimension_semantics=sem, vmem_limit_bytes=VMEM_LIMIT)


def _resident(shape, index_map):
    return pl.BlockSpec(shape, index_map, pipeline_mode=pl.Buffered(1))


def _layer_norm(x, g, b):
    mu = jnp.mean(x, axis=-1, keepdims=True)
    d = x - mu
    var = jnp.mean(d * d, axis=-1, keepdims=True)
    return d * lax.rsqrt(var + LN_EPS) * g + b


def _sigmoid(x):
    return 1.0 / (1.0 + jnp.exp(-x))


def _rope_proj_kernel(x_ref, w_ref, cos_ref, sa_ref, sb_ref, *out_refs):
    acc = jnp.dot(x_ref[...], w_ref[...], preferred_element_type=F32)
    width = acc.shape[1]
    reps = width // LANES
    cos = jnp.concatenate([cos_ref[...]] * reps, axis=1)
    sa = jnp.concatenate([sa_ref[...]] * reps, axis=1)
    sb = jnp.concatenate([sb_ref[...]] * reps, axis=1)
    half = ROT_DIM // 2
    y = (acc * cos + pltpu.roll(acc, width - half, 1) * sa
         + pltpu.roll(acc, half, 1) * sb)
    for o in out_refs:
        o[...] = y.astype(o.dtype)


def _plain_proj_kernel(x_ref, w_ref, *out_refs):
    acc = jnp.dot(x_ref[...], w_ref[...], preferred_element_type=F32)
    for o in out_refs:
        o[...] = acc.astype(o.dtype)


def _glu_proj_kernel(x_ref, wa_ref, wg_ref, o_ref):
    x = x_ref[...]
    a = jnp.dot(x, wa_ref[...], preferred_element_type=F32)
    g = jnp.dot(x, wg_ref[...], preferred_element_type=F32)
    o_ref[...] = a * _sigmoid(g)


def _sigmoid_proj_kernel(x_ref, w_ref, o_ref):
    acc = jnp.dot(x_ref[...], w_ref[...], preferred_element_type=F32)
    o_ref[...] = _sigmoid(acc)


def _proj_tiles(n, width):
    tm = _pick(n, (1024, 512, 256, 128, 64, 32, 16, 8))
    tn = _pick(width, (1024, 512, 256, 128))
    return tm, tn


def _rope_proj(xb, w, col0, width, tabs, out_dtypes):
    n, d = xb.shape
    tm, tn = _proj_tiles(n, width)
    tab_blocks = tabs[0].shape[0] // tm
    grid = (width // tn, n // tm)
    tab_spec = pl.BlockSpec((tm, LANES), lambda j, i: (i % tab_blocks, 0))
    return pl.pallas_call(
        _rope_proj_kernel,
        grid=grid,
        in_specs=[pl.BlockSpec((tm, d), lambda j, i: (i, 0)),
                  pl.BlockSpec((d, tn), lambda j, i: (0, col0 // tn + j)),
                  tab_spec, tab_spec, tab_spec],
        out_specs=[pl.BlockSpec((tm, tn), lambda j, i: (i, j)) for _ in out_dtypes],
        out_shape=[jax.ShapeDtypeStruct((n, width), dt) for dt in out_dtypes],
        compiler_params=_params("parallel", "parallel"),
        name="rope_proj",
    )(xb, w, *tabs)


def _plain_proj(xb, w, col0, width, out_dtypes):
    n, d = xb.shape
    tm, tn = _proj_tiles(n, width)
    return pl.pallas_call(
        _plain_proj_kernel,
        grid=(width // tn, n // tm),
        in_specs=[pl.BlockSpec((tm, d), lambda j, i: (i, 0)),
                  pl.BlockSpec((d, tn), lambda j, i: (0, col0 // tn + j))],
        out_specs=[pl.BlockSpec((tm, tn), lambda j, i: (i, j)) for _ in out_dtypes],
        out_shape=[jax.ShapeDtypeStruct((n, width), dt) for dt in out_dtypes],
        compiler_params=_params("parallel", "parallel"),
        name="plain_proj",
    )(xb, w)


def _glu_proj(xb, w, col0, width):
    n, d = xb.shape
    tm, tn = _proj_tiles(n, width)
    return pl.pallas_call(
        _glu_proj_kernel,
        grid=(width // tn, n // tm),
        in_specs=[pl.BlockSpec((tm, d), lambda j, i: (i, 0)),
                  pl.BlockSpec((d, tn), lambda j, i: (0, col0 // tn + j)),
                  pl.BlockSpec((d, tn), lambda j, i: (0, (col0 + width) // tn + j))],
        out_specs=pl.BlockSpec((tm, tn), lambda j, i: (i, j)),
        out_shape=jax.ShapeDtypeStruct((n, width), F32),
        compiler_params=_params("parallel", "parallel"),
        name="glu_proj",
    )(xb, w, w)


def _sigmoid_proj(xb, w, col0, width):
    n, d = xb.shape
    tm, tn = _proj_tiles(n, width)
    return pl.pallas_call(
        _sigmoid_proj_kernel,
        grid=(width // tn, n // tm),
        in_specs=[pl.BlockSpec((tm, d), lambda j, i: (i, 0)),
                  pl.BlockSpec((d, tn), lambda j, i: (0, col0 // tn + j))],
        out_specs=pl.BlockSpec((tm, tn), lambda j, i: (i, j)),
        out_shape=jax.ShapeDtypeStruct((n, width), F32),
        compiler_params=_params("parallel", "parallel"),
        name="gate_proj",
    )(xb, w)


def _lambda_value(lq_ref, lk_ref, lam_init):
    lf = lq_ref[...] * lk_ref[...]
    e = jnp.exp(jnp.sum(lf, axis=1, keepdims=True))
    return e[0:1, :] - e[1:2, :] + lam_init


def _finish_heads(o1, o2, lam, g_ref, lam_init):
    o = o1 - lam * o2
    ms = jnp.mean(o * o, axis=-1, keepdims=True)
    return o * lax.rsqrt(ms + LN_EPS) * g_ref[...] * (1.0 - lam_init)


def _attn_prompt_kernel(lq_ref, lk_ref, g_ref, q_ref, k_ref, v_ref, o_ref,
                        m1, l1, a1, m2, l2, a2, *, tq, lam_init):
    i = pl.program_id(2)
    c = (HEAD_DIM ** -0.5) * LOG2E
    q = q_ref[...]
    qs = (q[:, :HEAD_DIM], q[:, HEAD_DIM:])
    ms, ls, accs = (m1, m2), (l1, l2), (a1, a2)
    for r in ms:
        r[...] = jnp.full(r.shape, NEG_BIG, F32)
    for r in ls + accs:
        r[...] = jnp.zeros(r.shape, F32)

    def step(kb, mask):
        r0 = pl.multiple_of(kb * tq, tq)
        k = k_ref[pl.ds(r0, tq), :]
        v = v_ref[pl.ds(r0, tq), :]
        for comp in range(2):
            kc = k[:, comp * HEAD_DIM:(comp + 1) * HEAD_DIM]
            s = lax.dot_general(qs[comp], kc, (((1,), (1,)), ((), ())),
                                preferred_element_type=F32)
            if mask is not None:
                s = jnp.where(mask, s, NEG_BIG)
            m_prev = ms[comp][...]
            m_new = jnp.maximum(m_prev, jnp.max(s, axis=1, keepdims=True))
            alpha = jnp.exp2((m_prev - m_new) * c)
            p = jnp.exp2((s - jnp.concatenate([m_new] * (tq // LANES), axis=1)) * c)
            ls[comp][...] = alpha * ls[comp][...] + jnp.sum(p, axis=1, keepdims=True)
            accs[comp][...] = (accs[comp][...] * jnp.concatenate([alpha] * (HEAD_W // LANES), axis=1)
                               + jnp.dot(p.astype(BF16), v, preferred_element_type=F32))
            ms[comp][...] = m_new

    def body(kb, carry):
        step(kb, None)
        return carry

    lax.fori_loop(0, i, body, 0)
    qc = lax.broadcasted_iota(jnp.int32, (tq, tq), 0) // CHUNK
    kc = lax.broadcasted_iota(jnp.int32, (tq, tq), 1) // CHUNK
    step(i, kc <= qc)

    lam = _lambda_value(lq_ref, lk_ref, lam_init)
    rep = HEAD_W // LANES
    o1 = a1[...] / jnp.concatenate([l1[...]] * rep, axis=1)
    o2 = a2[...] / jnp.concatenate([l2[...]] * rep, axis=1)
    o_ref[...] = _finish_heads(o1, o2, lam, g_ref, lam_init).astype(o_ref.dtype)


def _attn_prompt(qb, kb, vb, lam_q, lam_k, subln_g, lam_init):
    b, s, w = qb.shape
    tq = _pick(s, (256, 128, 64))
    kern = functools.partial(_attn_prompt_kernel, tq=tq, lam_init=lam_init)
    small = lambda shape: pl.BlockSpec(shape, lambda bi, h, i: (0, 0))
    return pl.pallas_call(
        kern,
        grid=(b, N_HEADS, s // tq),
        in_specs=[small((2, HEAD_DIM)), small((2, HEAD_DIM)), small((1, HEAD_W)),
                  pl.BlockSpec((None, tq, HEAD_W), lambda bi, h, i: (bi, i, h)),
                  pl.BlockSpec((None, s, HEAD_W), lambda bi, h, i: (bi, 0, h)),
                  pl.BlockSpec((None, s, HEAD_W), lambda bi, h, i: (bi, 0, h))],
        out_specs=pl.BlockSpec((None, tq, HEAD_W), lambda bi, h, i: (bi, i, h)),
        out_shape=jax.ShapeDtypeStruct((b, s, w), BF16),
        scratch_shapes=[pltpu.VMEM((tq, LANES), F32), pltpu.VMEM((tq, LANES), F32),
                        pltpu.VMEM((tq, HEAD_W), F32),
                        pltpu.VMEM((tq, LANES), F32), pltpu.VMEM((tq, LANES), F32),
                        pltpu.VMEM((tq, HEAD_W), F32)],
        compiler_params=_params("parallel", "parallel", "arbitrary"),
        name="attn_prompt",
    )(lam_q, lam_k, subln_g.reshape(1, HEAD_W), qb, kb, vb)


def _attn_sample_kernel(lq_ref, lk_ref, g_ref, q_ref, kn_ref, vn_ref, kp_ref, vp_ref, o_ref,
                        *, lam_init):
    c = (HEAD_DIM ** -0.5) * LOG2E
    q = q_ref[...]
    kn = kn_ref[...]
    vn = vn_ref[...]
    kp = kp_ref[...].astype(BF16)
    vp = vp_ref[...].astype(BF16)
    outs = []
    for comp in range(2):
        sl = slice(comp * HEAD_DIM, (comp + 1) * HEAD_DIM)
        dn = (((1,), (1,)), ((), ()))
        sp = lax.dot_general(q[:, sl], kp[:, sl], dn, preferred_element_type=F32)
        sn = lax.dot_general(q[:, sl], kn[:, sl], dn, preferred_element_type=F32)
        m = jnp.maximum(jnp.max(sp, axis=1, keepdims=True), jnp.max(sn, axis=1, keepdims=True))
        pp = jnp.exp2((sp - m) * c)
        pn = jnp.exp2((sn - m) * c)
        l = jnp.sum(pp, axis=1, keepdims=True) + jnp.sum(pn, axis=1, keepdims=True)
        o = (jnp.dot(pp.astype(BF16), vp, preferred_element_type=F32)
             + jnp.dot(pn.astype(BF16), vn, preferred_element_type=F32))
        outs.append(o / l)
    lam = _lambda_value(lq_ref, lk_ref, lam_init)
    o_ref[...] = _finish_heads(outs[0], outs[1], lam, g_ref, lam_init).astype(o_ref.dtype)


def _attn_sample(qb, kb, vb, k_past, v_past, lam_q, lam_k, subln_g, lam_init):
    b, t, w = qb.shape
    p = k_past.shape[1]
    kern = functools.partial(_attn_sample_kernel, lam_init=lam_init)
    small = lambda shape: pl.BlockSpec(shape, lambda bi, h: (0, 0))
    new = pl.BlockSpec((None, t, HEAD_W), lambda bi, h: (bi, 0, h))
    past = pl.BlockSpec((None, p, HEAD_W), lambda bi, h: (bi, 0, h))
    return pl.pallas_call(
        kern,
        grid=(b, N_HEADS),
        in_specs=[small((2, HEAD_DIM)), small((2, HEAD_DIM)), small((1, HEAD_W)),
                  new, new, new, past, past],
        out_specs=new,
        out_shape=jax.ShapeDtypeStruct((b, t, w), BF16),
        compiler_params=_params("parallel", "parallel"),
        name="attn_sample",
    )(lam_q, lam_k, subln_g.reshape(1, HEAD_W), qb, kb, vb, k_past, v_past)


def _conv_kernel(*refs, tm, has_halo):
    if has_halo:
        past_ref, halo_ref, cur_ref, w_ref, b_ref, g_ref, beta_ref, o_ref, win_ref, y_ref = refs
    else:
        past_ref, cur_ref, w_ref, b_ref, g_ref, beta_ref, o_ref, win_ref, y_ref = refs
        halo_ref = None
    i = pl.program_id(1)
    if has_halo:
        @pl.when(i == 0)
        def _():
            win_ref[0:CONV_HALO, :] = past_ref[...]

        @pl.when(i > 0)
        def _():
            win_ref[0:CONV_HALO, :] = halo_ref[...]
    else:
        win_ref[0:CONV_HALO, :] = past_ref[...]
    win_ref[CONV_HALO:, :] = cur_ref[...]

    width = cur_ref.shape[1]
    rc = _pick(tm, (32, 16, 8))
    lead = CONV_HALO - (CONV_W - 1)

    def col_body(cj, carry):
        c0 = pl.multiple_of(cj * LANES, LANES)
        wk = [jnp.broadcast_to(w_ref[pl.ds(k, 1), pl.ds(c0, LANES)], (rc, LANES))
              for k in range(CONV_W)]
        bias = jnp.broadcast_to(b_ref[:, pl.ds(c0, LANES)], (rc, LANES))
        for r in range(tm // rc):
            acc = bias
            for k in range(CONV_W):
                acc = acc + win_ref[pl.ds(r * rc + lead + k, rc), pl.ds(c0, LANES)] * wk[k]
            y_ref[pl.ds(r * rc, rc), pl.ds(c0, LANES)] = acc
        return carry

    lax.fori_loop(0, width // LANES, col_body, 0)
    y = _layer_norm(y_ref[...], g_ref[...], beta_ref[...])
    o_ref[...] = (y * _sigmoid(y)).astype(o_ref.dtype)


def _conv_branch(c, past32, w_dw32, b_dw, cln_g, cln_b):
    b, t, ch = c.shape
    tm = _pick(t, (256, 128, 64, 32, 16, 8))
    has_halo = t > tm
    kern = functools.partial(_conv_kernel, tm=tm, has_halo=has_halo)
    vec = lambda: pl.BlockSpec((1, ch), lambda bi, i: (0, 0))
    hb = tm // CONV_HALO
    in_specs = [pl.BlockSpec((None, CONV_HALO, ch), lambda bi, i: (bi, 0, 0))]
    args = [past32]
    if has_halo:
        in_specs.append(pl.BlockSpec((None, CONV_HALO, ch),
                                     lambda bi, i: (bi, jnp.maximum(i * hb - 1, 0), 0)))
        args.append(c)
    in_specs += [pl.BlockSpec((None, tm, ch), lambda bi, i: (bi, i, 0)),
                 pl.BlockSpec((CONV_HALO, ch), lambda bi, i: (0, 0)),
                 vec(), vec(), vec()]
    args += [c, w_dw32, b_dw.reshape(1, ch), cln_g.reshape(1, ch), cln_b.reshape(1, ch)]
    return pl.pallas_call(
        kern,
        grid=(b, t // tm),
        in_specs=in_specs,
        out_specs=pl.BlockSpec((None, tm, ch), lambda bi, i: (bi, i, 0)),
        out_shape=jax.ShapeDtypeStruct((b, t, ch), BF16),
        scratch_shapes=[pltpu.VMEM((tm + CONV_HALO, ch), F32), pltpu.VMEM((tm, ch), F32)],
        compiler_params=_params("parallel", "parallel"),
        name="conv_branch",
    )(*args)


def _mix_kernel(on_ref, ca_ref, ga_ref, gc_ref, wa_ref, wc_ref, o_ref):
    att = jnp.dot(on_ref[...], wa_ref[...], preferred_element_type=F32)
    cv = jnp.dot(ca_ref[...], wc_ref[...], preferred_element_type=F32)
    o_ref[...] = (ga_ref[...] * att + gc_ref[...] * cv).astype(o_ref.dtype)


def _branch_mix(o_n, cact, gates, w_att, w_conv):
    n, d = o_n.shape
    dm = w_att.shape[1]
    tm = _pick(n, (512, 256, 128, 64, 32, 16, 8))
    tn = _pick(dm, (1024, 512, 256, 128))
    nj = dm // tn
    return pl.pallas_call(
        _mix_kernel,
        grid=(nj, n // tm),
        in_specs=[pl.BlockSpec((tm, d), lambda j, i: (i, 0)),
                  pl.BlockSpec((tm, cact.shape[1]), lambda j, i: (i, 0)),
                  pl.BlockSpec((tm, tn), lambda j, i: (i, j)),
                  pl.BlockSpec((tm, tn), lambda j, i: (i, nj + j)),
                  pl.BlockSpec((d, tn), lambda j, i: (0, j)),
                  pl.BlockSpec((cact.shape[1], tn), lambda j, i: (0, j))],
        out_specs=pl.BlockSpec((tm, tn), lambda j, i: (i, j)),
        out_shape=jax.ShapeDtypeStruct((n, dm), BF16),
        compiler_params=_params("parallel", "parallel"),
        name="branch_mix",
    )(o_n, cact, gates, gates, w_att, w_conv)


def _out_ln_kernel(mix_ref, x_ref, w_ref, g_ref, b_ref, o_ref, ob_ref, *, alpha):
    y = alpha * x_ref[...] + jnp.dot(mix_ref[...], w_ref[...], preferred_element_type=F32)
    y = _layer_norm(y, g_ref[...], b_ref[...])
    o_ref[...] = y
    ob_ref[...] = y.astype(BF16)


def _out_ln(mix, x, w_o, g, b, alpha):
    n, d = x.shape
    tm = _pick(n, (512, 256, 128, 64, 32, 16, 8))
    vec = lambda: pl.BlockSpec((1, d), lambda i: (0, 0))
    return pl.pallas_call(
        functools.partial(_out_ln_kernel, alpha=alpha),
        grid=(n // tm,),
        in_specs=[pl.BlockSpec((tm, d), lambda i: (i, 0)),
                  pl.BlockSpec((tm, d), lambda i: (i, 0)),
                  _resident((d, d), lambda i: (0, 0)),
                  vec(), vec()],
        out_specs=[pl.BlockSpec((tm, d), lambda i: (i, 0)),
                   pl.BlockSpec((tm, d), lambda i: (i, 0))],
        out_shape=[jax.ShapeDtypeStruct((n, d), F32), jax.ShapeDtypeStruct((n, d), BF16)],
        compiler_params=_params("parallel"),
        name="out_ln1",
    )(mix, x, w_o, g.reshape(1, d), b.reshape(1, d))


def _peer_candidate_plan():
    return [PEER_TOPK // (a + 1) for a in range(PEER_TOPK)]


def _top_rows(s, count):
    rows, t = s.shape
    ridx = lax.broadcasted_iota(jnp.int32, (count, t), 0)
    out = jnp.zeros((count, t), F32)
    cur = s
    for r in range(count):
        m = jnp.max(cur, axis=0, keepdims=True)
        out = jnp.where(ridx == r, m, out)
        if r + 1 < count:
            cur = jnp.where(cur == m, -jnp.inf, cur)
    return out


def _peer_route_kernel(x_ref, wq_ref, keys_ref, at_ref, bt_ref, e1_ref, e2_ref, tau_ref):
    q = jnp.dot(x_ref[...], wq_ref[...], preferred_element_type=F32).astype(BF16)
    tn = q.shape[0]
    limits = _peer_candidate_plan()
    row8 = lax.broadcasted_iota(jnp.int32, (8, tn), 0)
    for h in range(PEER_HEADS):
        st = []
        for half in range(2):
            col = (h * 2 + half) * PEER_NKEYS
            s = lax.dot_general(keys_ref[h, half], q[:, col:col + PEER_NKEYS],
                                (((1,), (1,)), ((), ())), preferred_element_type=F32)
            st.append(s * LOG2E)
        v1 = _top_rows(st[0], PEER_TOPK)
        v2 = _top_rows(st[1], PEER_TOPK)
        groups = [v1[0:1, :] + v2[0:8, :], v1[0:1, :] + v2[8:16, :]]
        for a in range(1, 8):
            cand = v1[a:a + 1, :] + v2[0:8, :]
            if limits[a] < 8:
                cand = jnp.where(row8 < limits[a], cand, -jnp.inf)
            groups.append(cand)
        groups.append(v1[8:16, :] + v2[0:1, :])
        cand = jnp.concatenate(groups, axis=0)
        top = _top_rows(cand, PEER_TOPK)
        z = jnp.sum(jnp.exp2(top - top[0:1, :]), axis=0, keepdims=True)
        at_ref[h] = st[0]
        bt_ref[h] = st[1]
        e1_ref[h] = jnp.exp2(st[0] - v1[0:1, :])
        e2_ref[h] = jnp.exp2(st[1] - v2[0:1, :]) / z
        tau_ref[h] = top[PEER_TOPK - 1:PEER_TOPK, :]


def _peer_route(xb, wq, keys):
    n, d = xb.shape
    tn = _pick(n, (256, 128))
    qw = wq.shape[1]
    tab = lambda: pl.BlockSpec((PEER_HEADS, PEER_NKEYS, tn), lambda i: (0, 0, i))
    tab_shape = jax.ShapeDtypeStruct((PEER_HEADS, PEER_NKEYS, n), F32)
    return pl.pallas_call(
        _peer_route_kernel,
        grid=(n // tn,),
        in_specs=[pl.BlockSpec((tn, d), lambda i: (i, 0)),
                  _resident((d, qw), lambda i: (0, 0)),
                  _resident(keys.shape, lambda i: (0, 0, 0, 0))],
        out_specs=[tab(), tab(), tab(), tab(),
                   pl.BlockSpec((PEER_HEADS, 1, tn), lambda i: (0, 0, i))],
        out_shape=[tab_shape, tab_shape, tab_shape, tab_shape,
                   jax.ShapeDtypeStruct((PEER_HEADS, 1, n), F32)],
        compiler_params=_params("parallel"),
        name="peer_route",
    )(xb, wq, keys)


def _gelu_tanh(x):
    return 0.5 * x * (1.0 + jnp.tanh(math.sqrt(2.0 / math.pi) * (x + 0.044715 * (x * x * x))))


def _peer_mix_kernel(x_ref, at_ref, bt_ref, e1_ref, e2_ref, tau_ref, u_ref, v_ref, g_ref, b_ref,
                     o_ref, xb_ref, acc_ref, ht_ref, act_ref, *, alpha, eb_size):
    eb = pl.program_id(1)
    tn = x_ref.shape[0]
    groups = eb_size // PEER_NKEYS
    lane_groups = tn // LANES

    @pl.when(eb == 0)
    def _():
        xb_ref[...] = x_ref[...].astype(BF16)
        acc_ref[...] = jnp.zeros(acc_ref.shape, F32)

    ht_ref[...] = lax.dot_general(u_ref[...], xb_ref[...], (((1,), (1,)), ((), ())),
                                  preferred_element_type=F32)

    row8 = pl.multiple_of((eb * groups // 8) * 8, 8)
    off = (eb * groups) % 8

    def block_rows(ref, h, cs):
        grp = ref[h, pl.ds(row8, 8), cs]
        sel = grp[0:groups]
        for o in range(groups, 8, groups):
            sel = jnp.where(off == o, grp[o:o + groups], sel)
        return sel

    def lane_body(tc, carry):
        cs = pl.ds(pl.multiple_of(tc * LANES, LANES), LANES)
        a_rows = [block_rows(at_ref, h, cs) for h in range(PEER_HEADS)]
        e_rows = [block_rows(e1_ref, h, cs) for h in range(PEER_HEADS)]
        for ii in range(groups):
            gsum = jnp.zeros((PEER_NKEYS, LANES), F32)
            for h in range(PEER_HEADS):
                total = a_rows[h][ii:ii + 1] + bt_ref[h, :, cs]
                gate = e_rows[h][ii:ii + 1] * e2_ref[h, :, cs]
                gsum = gsum + jnp.where(total >= tau_ref[h, :, cs], gate, 0.0)
            rs = slice(ii * PEER_NKEYS, (ii + 1) * PEER_NKEYS)
            act_ref[rs, cs] = (_gelu_tanh(ht_ref[rs, cs]) * gsum).astype(BF16)
        return carry

    lax.fori_loop(0, lane_groups, lane_body, 0)

    acc_ref[...] += lax.dot_general(act_ref[...], v_ref[...], (((0,), (0,)), ((), ())),
                                    preferred_element_type=F32)

    @pl.when(eb == pl.num_programs(1) - 1)
    def _():
        y = alpha * x_ref[...] + acc_ref[...]
        o_ref[...] = _layer_norm(y, g_ref[...], b_ref[...])


def _peer_mix(x, tabs, u, v, g, b, alpha):
    n, d = x.shape
    e = u.shape[0]
    at, bt, e1, e2, tau = tabs
    tn = _pick(n, (512, 256, 128))
    eb_size = _pick(e, (512, 256, 128))
    tab = lambda: pl.BlockSpec((PEER_HEADS, PEER_NKEYS, tn), lambda t, j: (0, 0, t))
    vec = lambda: pl.BlockSpec((1, d), lambda t, j: (0, 0))
    kern = functools.partial(_peer_mix_kernel, alpha=alpha, eb_size=eb_size)
    return pl.pallas_call(
        kern,
        grid=(n // tn, e // eb_size),
        in_specs=[pl.BlockSpec((tn, d), lambda t, j: (t, 0)),
                  tab(), tab(), tab(), tab(),
                  pl.BlockSpec((PEER_HEADS, 1, tn), lambda t, j: (0, 0, t)),
                  pl.BlockSpec((eb_size, d), lambda t, j: (j, 0)),
                  pl.BlockSpec((eb_size, d), lambda t, j: (j, 0)),
                  vec(), vec()],
        out_specs=pl.BlockSpec((tn, d), lambda t, j: (t, 0)),
        out_shape=jax.ShapeDtypeStruct((n, d), F32),
        scratch_shapes=[pltpu.VMEM((tn, d), BF16), pltpu.VMEM((tn, d), F32),
                        pltpu.VMEM((eb_size, tn), F32), pltpu.VMEM((eb_size, tn), BF16)],
        compiler_params=_params("parallel", "arbitrary"),
        name="peer_mix",
    )(x, at, bt, e1, e2, tau, u, v, g.reshape(1, d), b.reshape(1, d))


def _ple_kernel(xr_ref, xc_ref, pe_ref, wg_ref, wp_ref, o_ref):
    gate = _sigmoid(jnp.dot(xr_ref[...].astype(BF16), wg_ref[...], preferred_element_type=F32))
    emb = jnp.dot(pe_ref[...].astype(BF16), wp_ref[...], preferred_element_type=F32)
    o_ref[...] = xc_ref[...] + gate * emb


def _ple(x, pe, w_gate, w_ple):
    n, d = x.shape
    pd = pe.shape[1]
    tm = _pick(n, (512, 256, 128, 64, 32, 16, 8))
    tn = _pick(d, (1024, 512, 256, 128))
    return pl.pallas_call(
        _ple_kernel,
        grid=(d // tn, n // tm),
        in_specs=[pl.BlockSpec((tm, d), lambda j, i: (i, 0)),
                  pl.BlockSpec((tm, tn), lambda j, i: (i, j)),
                  pl.BlockSpec((tm, pd), lambda j, i: (i, 0)),
                  pl.BlockSpec((d, tn), lambda j, i: (0, j)),
                  pl.BlockSpec((pd, tn), lambda j, i: (0, j))],
        out_specs=pl.BlockSpec((tm, tn), lambda j, i: (i, j)),
        out_shape=jax.ShapeDtypeStruct((n, d), F32),
        compiler_params=_params("parallel", "parallel"),
        name="ple",
    )(x, x, pe, w_gate, w_ple)


def _rope_tables(pos, rows_per_block_hint):
    inv = ROPE_THETA ** (-jnp.arange(0, ROT_DIM, 2, dtype=F32) / ROT_DIM)
    ang = pos.astype(F32)[:, None] * inv[None, :]
    cos, sin = jnp.cos(ang), jnp.sin(ang)
    t = pos.shape[0]
    half = ROT_DIM // 2
    ones = jnp.ones((t, HEAD_DIM - ROT_DIM), F32)
    zeros_h = jnp.zeros((t, half), F32)
    zeros_r = jnp.zeros((t, HEAD_DIM - ROT_DIM), F32)
    c = jnp.concatenate([cos, cos, ones], axis=1)
    sa = jnp.concatenate([-sin, zeros_h, zeros_r], axis=1)
    sb = jnp.concatenate([zeros_h, sin, zeros_r], axis=1)
    if t < rows_per_block_hint:
        reps = rows_per_block_hint // t
        c, sa, sb = (jnp.tile(a, (reps, 1)) for a in (c, sa, sb))
    return c, sa, sb


def _layer(x, pe, pos, k_past, v_past, conv_past, layer_idx, depth, wts):
    (w_in, lam_q, lam_k, subln_g, w_att_out, w_dw, b_dw, cln_g, cln_b, w_conv_out, w_o,
     ln1_g, ln1_b, peer_wq, peer_keys, peer_u, peer_v, ln2_g, ln2_b, w_ple, w_ple_gate) = wts
    b, t, d = x.shape
    n = b * t
    att_w = N_HEADS * HEAD_W
    conv_dim = w_dw.shape[1]
    alpha = (2 * depth) ** 0.25
    lam_init = 0.8 - 0.6 * math.exp(-0.3 * layer_idx)

    x2d = x.reshape(n, d)
    xb = x2d.astype(BF16)
    tm, _ = _proj_tiles(n, att_w)
    tabs = _rope_tables(pos, tm)

    (qb,) = _rope_proj(xb, w_in, 0, att_w, tabs, (BF16,))
    k_f32, kb = _rope_proj(xb, w_in, att_w, att_w, tabs, (F32, BF16))
    v_f32, vb = _plain_proj(xb, w_in, 2 * att_w, att_w, (F32, BF16))
    c = _glu_proj(xb, w_in, 3 * att_w, conv_dim)
    gates = _sigmoid_proj(xb, w_in, 3 * att_w + 2 * conv_dim, 2 * d)

    shp = (b, t, att_w)
    if k_past is None:
        o_n = _attn_prompt(qb.reshape(shp), kb.reshape(shp), vb.reshape(shp),
                           lam_q, lam_k, subln_g, lam_init)
    else:
        p = k_past.shape[1]
        o_n = _attn_sample(qb.reshape(shp), kb.reshape(shp), vb.reshape(shp),
                           k_past.reshape(b, p, att_w), v_past.reshape(b, p, att_w),
                           lam_q, lam_k, subln_g, lam_init)

    c3 = c.reshape(b, t, conv_dim)
    past32 = jnp.pad(conv_past, ((0, 0), (CONV_HALO - (CONV_W - 1), 0), (0, 0)))
    w_dw32 = jnp.pad(w_dw, ((0, CONV_HALO - CONV_W), (0, 0)))
    cact = _conv_branch(c3, past32, w_dw32, b_dw, cln_g, cln_b)
    conv_state = jnp.concatenate([conv_past, c3], axis=1)[:, -(CONV_W - 1):]

    mix = _branch_mix(o_n.reshape(n, att_w), cact.reshape(n, conv_dim), gates, w_att_out, w_conv_out)
    x1, x1b = _out_ln(mix, x2d, w_o, ln1_g, ln1_b, alpha)

    route = _peer_route(x1b, peer_wq, peer_keys)
    x2 = _peer_mix(x1, route, peer_u, peer_v, ln2_g, ln2_b, alpha)

    y = _ple(x2, pe.reshape(n, pe.shape[-1]), w_ple_gate, w_ple)
    return (y.reshape(b, t, d), k_f32.reshape(b, t, N_HEADS, HEAD_W),
            v_f32.reshape(b, t, N_HEADS, HEAD_W), conv_state)


def kernel(x_prompt, x_sample, cache_k, cache_v, state_conv, p_prompt, p_sample, w_in, lam_q, lam_k, subln_g, w_att_out, w_dw, b_dw, cln_g, cln_b, w_conv_out, w_o, ln1_g, ln1_b, peer_wq, peer_keys, peer_u, peer_v, ln2_g, ln2_b, w_ple, w_ple_gate):
    depth = w_in.shape[0]
    xp, xs = x_prompt, x_sample
    pos_p = jnp.arange(x_prompt.shape[1])
    pos_s = cache_k.shape[2] + jnp.arange(x_sample.shape[1])
    mats = (w_in, w_att_out, w_conv_out, w_o, peer_wq, peer_keys, peer_u, peer_v, w_ple, w_ple_gate)
    w_in_b, w_att_b, w_conv_b, w_o_b, wq_b, keys_b, u_b, v_b, w_ple_b, w_pg_b = (
        m.astype(BF16) for m in mats)
    outs = [[] for _ in range(6)]
    for l in range(depth):
        wts = (w_in_b[l], lam_q[l], lam_k[l], subln_g[l], w_att_b[l], w_dw[l], b_dw[l],
               cln_g[l], cln_b[l], w_conv_b[l], w_o_b[l], ln1_g[l], ln1_b[l],
               wq_b[l], keys_b[l], u_b[l], v_b[l], ln2_g[l], ln2_b[l], w_ple_b[l], w_pg_b[l])
        zero_conv = jnp.zeros((xp.shape[0], CONV_W - 1, w_dw.shape[2]), xp.dtype)
        xp, kp, vp, cp = _layer(xp, p_prompt[l], pos_p, None, None, zero_conv, l, depth, wts)
        xs, ks, vs, cs = _layer(xs, p_sample[l], pos_s, cache_k[l], cache_v[l], state_conv[l],
                                l, depth, wts)
        for lst, val in zip(outs, (kp, vp, cp, ks, vs, cs)):
            lst.append(val)
    return (xp, xs) + tuple(jnp.stack(o) for o in outs)
```

```python
import functools
import math

import jax
import jax.numpy as jnp
from jax import lax
from jax.experimental import pallas as pl
from jax.experimental.pallas import tpu as pltpu

F32 = jnp.float32
BF16 = jnp.bfloat16

CHUNK = 64
N_HEADS = 8
HEAD_DIM = 128
HEAD_W = 2 * HEAD_DIM
ROT_DIM = HEAD_DIM // 4
ROPE_THETA = 500000.0
CONV_W = 31
CONV_HALO = 32
PEER_HEADS = 8
PEER_NKEYS = 128
PEER_TOPK = 16
LN_EPS = 1e-5
LOG2E = math.log2(math.e)
NEG_BIG = -1e30
LANES = 128
VMEM_LIMIT = 56 * 2**20


def _pick(n, prefs):
    for p in prefs:
        if n % p == 0:
            return p
    return n


def _params(*sem):
    return pltpu.CompilerParams(dimension_semantics=sem, vmem_limit_bytes=VMEM_LIMIT)


def _resident(shape, index_map):
    return pl.BlockSpec(shape, index_map, pipeline_mode=pl.Buffered(1))


def _layer_norm(x, g, b):
    mu = jnp.mean(x, axis=-1, keepdims=True)
    d = x - mu
    var = jnp.mean(d * d, axis=-1, keepdims=True)
    return d * lax.rsqrt(var + LN_EPS) * g + b


def _sigmoid(x):
    return 1.0 / (1.0 + jnp.exp(-x))


def _rope_proj_kernel(x_ref, w_ref, cos_ref, sa_ref, sb_ref, *out_refs):
    acc = jnp.dot(x_ref[...], w_ref[...], preferred_element_type=F32)
    width = acc.shape[1]
    reps = width // LANES
    cos = jnp.concatenate([cos_ref[...]] * reps, axis=1)
    sa = jnp.concatenate([sa_ref[...]] * reps, axis=1)
    sb = jnp.concatenate([sb_ref[...]] * reps, axis=1)
    half = ROT_DIM // 2
    y = (acc * cos + pltpu.roll(acc, width - half, 1) * sa
         + pltpu.roll(acc, half, 1) * sb)
    for o in out_refs:
        o[...] = y.astype(o.dtype)


def _plain_proj_kernel(x_ref, w_ref, *out_refs):
    acc = jnp.dot(x_ref[...], w_ref[...], preferred_element_type=F32)
    for o in out_refs:
        o[...] = acc.astype(o.dtype)


def _glu_proj_kernel(x_ref, wa_ref, wg_ref, o_ref):
    x = x_ref[...]
    a = jnp.dot(x, wa_ref[...], preferred_element_type=F32)
    g = jnp.dot(x, wg_ref[...], preferred_element_type=F32)
    o_ref[...] = a * _sigmoid(g)


def _sigmoid_proj_kernel(x_ref, w_ref, o_ref):
    acc = jnp.dot(x_ref[...], w_ref[...], preferred_element_type=F32)
    o_ref[...] = _sigmoid(acc)


def _proj_tiles(n, width):
    tm = _pick(n, (1024, 512, 256, 128, 64, 32, 16, 8))
    tn = _pick(width, (1024, 512, 256, 128))
    return tm, tn


def _rope_proj(xb, w, col0, width, tabs, out_dtypes):
    n, d = xb.shape
    tm, tn = _proj_tiles(n, width)
    tab_blocks = tabs[0].shape[0] // tm
    grid = (width // tn, n // tm)
    tab_spec = pl.BlockSpec((tm, LANES), lambda j, i: (i % tab_blocks, 0))
    return pl.pallas_call(
        _rope_proj_kernel,
        grid=grid,
        in_specs=[pl.BlockSpec((tm, d), lambda j, i: (i, 0)),
                  pl.BlockSpec((d, tn), lambda j, i: (0, col0 // tn + j)),
                  tab_spec, tab_spec, tab_spec],
        out_specs=[pl.BlockSpec((tm, tn), lambda j, i: (i, j)) for _ in out_dtypes],
        out_shape=[jax.ShapeDtypeStruct((n, width), dt) for dt in out_dtypes],
        compiler_params=_params("parallel", "parallel"),
        name="rope_proj",
    )(xb, w, *tabs)


def _plain_proj(xb, w, col0, width, out_dtypes):
    n, d = xb.shape
    tm, tn = _proj_tiles(n, width)
    return pl.pallas_call(
        _plain_proj_kernel,
        grid=(width // tn, n // tm),
        in_specs=[pl.BlockSpec((tm, d), lambda j, i: (i, 0)),
                  pl.BlockSpec((d, tn), lambda j, i: (0, col0 // tn + j))],
        out_specs=[pl.BlockSpec((tm, tn), lambda j, i: (i, j)) for _ in out_dtypes],
        out_shape=[jax.ShapeDtypeStruct((n, width), dt) for dt in out_dtypes],
        compiler_params=_params("parallel", "parallel"),
        name="plain_proj",
    )(xb, w)


def _glu_proj(xb, w, col0, width):
    n, d = xb.shape
    tm, tn = _proj_tiles(n, width)
    return pl.pallas_call(
        _glu_proj_kernel,
        grid=(width // tn, n // tm),
        in_specs=[pl.BlockSpec((tm, d), lambda j, i: (i, 0)),
                  pl.BlockSpec((d, tn), lambda j, i: (0, col0 // tn + j)),
                  pl.BlockSpec((d, tn), lambda j, i: (0, (col0 + width) // tn + j))],
        out_specs=pl.BlockSpec((tm, tn), lambda j, i: (i, j)),
        out_shape=jax.ShapeDtypeStruct((n, width), F32),
        compiler_params=_params("parallel", "parallel"),
        name="glu_proj",
    )(xb, w, w)


def _sigmoid_proj(xb, w, col0, width):
    n, d = xb.shape
    tm, tn = _proj_tiles(n, width)
    return pl.pallas_call(
        _sigmoid_proj_kernel,
        grid=(width // tn, n // tm),
        in_specs=[pl.BlockSpec((tm, d), lambda j, i: (i, 0)),
                  pl.BlockSpec((d, tn), lambda j, i: (0, col0 // tn + j))],
        out_specs=pl.BlockSpec((tm, tn), lambda j, i: (i, j)),
        out_shape=jax.ShapeDtypeStruct((n, width), F32),
        compiler_params=_params("parallel", "parallel"),
        name="gate_proj",
    )(xb, w)


def _lambda_value(lq_ref, lk_ref, lam_init):
    lf = lq_ref[...] * lk_ref[...]
    e = jnp.exp(jnp.sum(lf, axis=1, keepdims=True))
    return e[0:1, :] - e[1:2, :] + lam_init


def _finish_heads(o1, o2, lam, g_ref, lam_init):
    o = o1 - lam * o2
    ms = jnp.mean(o * o, axis=-1, keepdims=True)
    return o * lax.rsqrt(ms + LN_EPS) * g_ref[...] * (1.0 - lam_init)


def _attn_prompt_kernel(lq_ref, lk_ref, g_ref, q_ref, k_ref, v_ref, o_ref,
                        m1, l1, a1, m2, l2, a2, *, tq, lam_init):
    i = pl.program_id(2)
    c = (HEAD_DIM ** -0.5) * LOG2E
    q = q_ref[...]
    qs = (q[:, :HEAD_DIM], q[:, HEAD_DIM:])
    ms, ls, accs = (m1, m2), (l1, l2), (a1, a2)
    for r in ms:
        r[...] = jnp.full(r.shape, NEG_BIG, F32)
    for r in ls + accs:
        r[...] = jnp.zeros(r.shape, F32)

    def step(kb, mask):
        r0 = pl.multiple_of(kb * tq, tq)
        k = k_ref[pl.ds(r0, tq), :]
        v = v_ref[pl.ds(r0, tq), :]
        for comp in range(2):
            kc = k[:, comp * HEAD_DIM:(comp + 1) * HEAD_DIM]
            s = lax.dot_general(qs[comp], kc, (((1,), (1,)), ((), ())),
                                preferred_element_type=F32)
            if mask is not None:
                s = jnp.where(mask, s, NEG_BIG)
            m_prev = ms[comp][...]
            m_new = jnp.maximum(m_prev, jnp.max(s, axis=1, keepdims=True))
            alpha = jnp.exp2((m_prev - m_new) * c)
            p = jnp.exp2((s - jnp.concatenate([m_new] * (tq // LANES), axis=1)) * c)
            ls[comp][...] = alpha * ls[comp][...] + jnp.sum(p, axis=1, keepdims=True)
            accs[comp][...] = (accs[comp][...] * jnp.concatenate([alpha] * (HEAD_W // LANES), axis=1)
                               + jnp.dot(p.astype(BF16), v, preferred_element_type=F32))
            ms[comp][...] = m_new

    def body(kb, carry):
        step(kb, None)
        return carry

    lax.fori_loop(0, i, body, 0)
    qc = lax.broadcasted_iota(jnp.int32, (tq, tq), 0) // CHUNK
    kc = lax.broadcasted_iota(jnp.int32, (tq, tq), 1) // CHUNK
    step(i, kc <= qc)

    lam = _lambda_value(lq_ref, lk_ref, lam_init)
    rep = HEAD_W // LANES
    o1 = a1[...] / jnp.concatenate([l1[...]] * rep, axis=1)
    o2 = a2[...] / jnp.concatenate([l2[...]] * rep, axis=1)
    o_ref[...] = _finish_heads(o1, o2, lam, g_ref, lam_init).astype(o_ref.dtype)


def _attn_prompt(qb, kb, vb, lam_q, lam_k, subln_g, lam_init):
    b, s, w = qb.shape
    tq = _pick(s, (512, 256, 128, 64))
    kern = functools.partial(_attn_prompt_kernel, tq=tq, lam_init=lam_init)
    small = lambda shape: pl.BlockSpec(shape, lambda bi, h, i: (0, 0))
    return pl.pallas_call(
        kern,
        grid=(b, N_HEADS, s // tq),
        in_specs=[small((2, HEAD_DIM)), small((2, HEAD_DIM)), small((1, HEAD_W)),
                  pl.BlockSpec((None, tq, HEAD_W), lambda bi, h, i: (bi, i, h)),
                  pl.BlockSpec((None, s, HEAD_W), lambda bi, h, i: (bi, 0, h)),
                  pl.BlockSpec((None, s, HEAD_W), lambda bi, h, i: (bi, 0, h))],
        out_specs=pl.BlockSpec((None, tq, HEAD_W), lambda bi, h, i: (bi, i, h)),
        out_shape=jax.ShapeDtypeStruct((b, s, w), BF16),
        scratch_shapes=[pltpu.VMEM((tq, LANES), F32), pltpu.VMEM((tq, LANES), F32),
                        pltpu.VMEM((tq, HEAD_W), F32),
                        pltpu.VMEM((tq, LANES), F32), pltpu.VMEM((tq, LANES), F32),
                        pltpu.VMEM((tq, HEAD_W), F32)],
        compiler_params=_params("parallel", "parallel", "arbitrary"),
        name="attn_prompt",
    )(lam_q, lam_k, subln_g.reshape(1, HEAD_W), qb, kb, vb)


def _attn_sample_kernel(lq_ref, lk_ref, g_ref, q_ref, kn_ref, vn_ref, kp_ref, vp_ref, o_ref,
                        qbd_ref, bias_ref, m_ref, l_ref, acc_ref, *, lam_init, t):
    j = pl.program_id(1)
    c = (HEAD_DIM ** -0.5) * LOG2E
    rows = N_HEADS * 2 * t

    def attend(k_flat, v_flat):
        n = k_flat.shape[0]
        s = lax.dot_general(qbd_ref[...], k_flat, (((1,), (1,)), ((), ())),
                            preferred_element_type=F32)
        s = s + jnp.concatenate([bias_ref[...]] * (n // LANES), axis=1)
        m_prev = m_ref[...]
        m_new = jnp.maximum(m_prev, jnp.max(s, axis=1, keepdims=True))
        alpha = jnp.exp2((m_prev - m_new) * c)
        p = jnp.exp2((s - jnp.concatenate([m_new] * (n // LANES), axis=1)) * c)
        l_ref[...] = alpha * l_ref[...] + jnp.sum(p, axis=1, keepdims=True)
        acc_ref[...] = (acc_ref[...] * jnp.concatenate([alpha] * (HEAD_W // LANES), axis=1)
                        + jnp.dot(p.astype(BF16), v_flat, preferred_element_type=F32))
        m_ref[...] = m_new

    @pl.when(j == 0)
    def _():
        qbd_ref[...] = jnp.zeros(qbd_ref.shape, BF16)
        for h in range(N_HEADS):
            for comp in range(2):
                r0 = (h * 2 + comp) * t
                cols = slice(comp * HEAD_DIM, (comp + 1) * HEAD_DIM)
                qbd_ref[r0:r0 + t, cols] = q_ref[:, h * HEAD_W + comp * HEAD_DIM:
                                                 h * HEAD_W + (comp + 1) * HEAD_DIM]
        row_head = lax.broadcasted_iota(jnp.int32, (rows, LANES), 0) // (2 * t)
        key_head = lax.broadcasted_iota(jnp.int32, (rows, LANES), 1) % N_HEADS
        bias_ref[...] = jnp.where(row_head == key_head, 0.0, NEG_BIG)
        m_ref[...] = jnp.full(m_ref.shape, NEG_BIG, F32)
        l_ref[...] = jnp.zeros(l_ref.shape, F32)
        acc_ref[...] = jnp.zeros(acc_ref.shape, F32)
        attend(kn_ref[...].reshape(t * N_HEADS, HEAD_W).astype(BF16),
               vn_ref[...].reshape(t * N_HEADS, HEAD_W).astype(BF16))

    pb = kp_ref.shape[0]
    attend(kp_ref[...].reshape(pb * N_HEADS, HEAD_W).astype(BF16),
           vp_ref[...].reshape(pb * N_HEADS, HEAD_W).astype(BF16))

    @pl.when(j == pl.num_programs(1) - 1)
    def _():
        lam = _lambda_value(lq_ref, lk_ref, lam_init)
        rep = HEAD_W // LANES
        o = acc_ref[...] / jnp.concatenate([l_ref[...]] * rep, axis=1)
        for h in range(N_HEADS):
            r0 = h * 2 * t
            o_ref[:, h * HEAD_W:(h + 1) * HEAD_W] = _finish_heads(
                o[r0:r0 + t], o[r0 + t:r0 + 2 * t], lam, g_ref, lam_init).astype(o_ref.dtype)


def _attn_sample(qb, k_new, v_new, k_cache, v_cache, layer_idx, lam_q, lam_k, subln_g, lam_init):
    b, t, w = qb.shape
    p = k_cache.shape[2]
    pb = _pick(p, (512, 256, 128, 64, 32, 16))
    rows = N_HEADS * 2 * t
    kern = functools.partial(_attn_sample_kernel, lam_init=lam_init, t=t)
    small = lambda shape: pl.BlockSpec(shape, lambda bi, j: (0, 0))
    new = pl.BlockSpec((None, t, N_HEADS, HEAD_W), lambda bi, j: (bi, 0, 0, 0))
    past = pl.BlockSpec((None, None, pb, N_HEADS, HEAD_W), lambda bi, j: (layer_idx, bi, j, 0, 0))
    return pl.pallas_call(
        kern,
        grid=(b, p // pb),
        in_specs=[small((2, HEAD_DIM)), small((2, HEAD_DIM)), small((1, HEAD_W)),
                  pl.BlockSpec((None, t, w), lambda bi, j: (bi, 0, 0)),
                  new, new, past, past],
        out_specs=pl.BlockSpec((None, t, w), lambda bi, j: (bi, 0, 0)),
        out_shape=jax.ShapeDtypeStruct((b, t, w), BF16),
        scratch_shapes=[pltpu.VMEM((rows, HEAD_W), BF16), pltpu.VMEM((rows, LANES), F32),
                        pltpu.VMEM((rows, LANES), F32), pltpu.VMEM((rows, LANES), F32),
                        pltpu.VMEM((rows, HEAD_W), F32)],
        compiler_params=_params("parallel", "arbitrary"),
        name="attn_sample",
    )(lam_q, lam_k, subln_g.reshape(1, HEAD_W), qb, k_new, v_new, k_cache, v_cache)


def _conv_kernel(*refs, tm, has_halo):
    if has_halo:
        past_ref, halo_ref, cur_ref, w_ref, b_ref, g_ref, beta_ref, o_ref, win_ref, y_ref = refs
    else:
        past_ref, cur_ref, w_ref, b_ref, g_ref, beta_ref, o_ref, win_ref, y_ref = refs
        halo_ref = None
    i = pl.program_id(1)
    if has_halo:
        @pl.when(i == 0)
        def _():
            win_ref[0:CONV_HALO, :] = past_ref[...]

        @pl.when(i > 0)
        def _():
            win_ref[0:CONV_HALO, :] = halo_ref[...]
    else:
        win_ref[0:CONV_HALO, :] = past_ref[...]
    win_ref[CONV_HALO:, :] = cur_ref[...]

    width = cur_ref.shape[1]
    rc = _pick(tm, (32, 16, 8))
    lead = CONV_HALO - (CONV_W - 1)

    def col_body(cj, carry):
        c0 = pl.multiple_of(cj * LANES, LANES)
        wk = [jnp.broadcast_to(w_ref[pl.ds(k, 1), pl.ds(c0, LANES)], (rc, LANES))
              for k in range(CONV_W)]
        bias = jnp.broadcast_to(b_ref[:, pl.ds(c0, LANES)], (rc, LANES))
        for r in range(tm // rc):
            acc = bias
            for k in range(CONV_W):
                acc = acc + win_ref[pl.ds(r * rc + lead + k, rc), pl.ds(c0, LANES)] * wk[k]
            y_ref[pl.ds(r * rc, rc), pl.ds(c0, LANES)] = acc
        return carry

    lax.fori_loop(0, width // LANES, col_body, 0)
    y = _layer_norm(y_ref[...], g_ref[...], beta_ref[...])
    o_ref[...] = (y * _sigmoid(y)).astype(o_ref.dtype)


def _conv_branch(c, past32, w_dw32, b_dw, cln_g, cln_b):
    b, t, ch = c.shape
    tm = _pick(t, (256, 128, 64, 32, 16, 8))
    has_halo = t > tm
    kern = functools.partial(_conv_kernel, tm=tm, has_halo=has_halo)
    vec = lambda: pl.BlockSpec((1, ch), lambda bi, i: (0, 0))
    hb = tm // CONV_HALO
    in_specs = [pl.BlockSpec((None, CONV_HALO, ch), lambda bi, i: (bi, 0, 0))]
    args = [past32]
    if has_halo:
        in_specs.append(pl.BlockSpec((None, CONV_HALO, ch),
                                     lambda bi, i: (bi, jnp.maximum(i * hb - 1, 0), 0)))
        args.append(c)
    in_specs += [pl.BlockSpec((None, tm, ch), lambda bi, i: (bi, i, 0)),
                 pl.BlockSpec((CONV_HALO, ch), lambda bi, i: (0, 0)),
                 vec(), vec(), vec()]
    args += [c, w_dw32, b_dw.reshape(1, ch), cln_g.reshape(1, ch), cln_b.reshape(1, ch)]
    return pl.pallas_call(
        kern,
        grid=(b, t // tm),
        in_specs=in_specs,
        out_specs=pl.BlockSpec((None, tm, ch), lambda bi, i: (bi, i, 0)),
        out_shape=jax.ShapeDtypeStruct((b, t, ch), BF16),
        scratch_shapes=[pltpu.VMEM((tm + CONV_HALO, ch), F32), pltpu.VMEM((tm, ch), F32)],
        compiler_params=_params("parallel", "parallel"),
        name="conv_branch",
    )(*args)


def _mix_kernel(on_ref, ca_ref, ga_ref, gc_ref, wa_ref, wc_ref, o_ref):
    att = jnp.dot(on_ref[...], wa_ref[...], preferred_element_type=F32)
    cv = jnp.dot(ca_ref[...], wc_ref[...], preferred_element_type=F32)
    o_ref[...] = (ga_ref[...] * att + gc_ref[...] * cv).astype(o_ref.dtype)


def _branch_mix(o_n, cact, gates, w_att, w_conv):
    n, d = o_n.shape
    dm = w_att.shape[1]
    tm = _pick(n, (512, 256, 128, 64, 32, 16, 8))
    tn = _pick(dm, (1024, 512, 256, 128))
    nj = dm // tn
    return pl.pallas_call(
        _mix_kernel,
        grid=(nj, n // tm),
        in_specs=[pl.BlockSpec((tm, d), lambda j, i: (i, 0)),
                  pl.BlockSpec((tm, cact.shape[1]), lambda j, i: (i, 0)),
                  pl.BlockSpec((tm, tn), lambda j, i: (i, j)),
                  pl.BlockSpec((tm, tn), lambda j, i: (i, nj + j)),
                  pl.BlockSpec((d, tn), lambda j, i: (0, j)),
                  pl.BlockSpec((cact.shape[1], tn), lambda j, i: (0, j))],
        out_specs=pl.BlockSpec((tm, tn), lambda j, i: (i, j)),
        out_shape=jax.ShapeDtypeStruct((n, dm), BF16),
        compiler_params=_params("parallel", "parallel"),
        name="branch_mix",
    )(o_n, cact, gates, gates, w_att, w_conv)


def _out_ln_kernel(mix_ref, x_ref, w_ref, g_ref, b_ref, o_ref, ob_ref, *, alpha):
    y = alpha * x_ref[...] + jnp.dot(mix_ref[...], w_ref[...], preferred_element_type=F32)
    y = _layer_norm(y, g_ref[...], b_ref[...])
    o_ref[...] = y
    ob_ref[...] = y.astype(BF16)


def _out_ln(mix, x, w_o, g, b, alpha):
    n, d = x.shape
    tm = _pick(n, (512, 256, 128, 64, 32, 16, 8))
    vec = lambda: pl.BlockSpec((1, d), lambda i: (0, 0))
    return pl.pallas_call(
        functools.partial(_out_ln_kernel, alpha=alpha),
        grid=(n // tm,),
        in_specs=[pl.BlockSpec((tm, d), lambda i: (i, 0)),
                  pl.BlockSpec((tm, d), lambda i: (i, 0)),
                  _resident((d, d), lambda i: (0, 0)),
                  vec(), vec()],
        out_specs=[pl.BlockSpec((tm, d), lambda i: (i, 0)),
                   pl.BlockSpec((tm, d), lambda i: (i, 0))],
        out_shape=[jax.ShapeDtypeStruct((n, d), F32), jax.ShapeDtypeStruct((n, d), BF16)],
        compiler_params=_params("parallel"),
        name="out_ln1",
    )(mix, x, w_o, g.reshape(1, d), b.reshape(1, d))


def _peer_candidate_plan():
    return [PEER_TOPK // (a + 1) for a in range(PEER_TOPK)]


def _top_rows(s, count):
    rows, t = s.shape
    ridx = lax.broadcasted_iota(jnp.int32, (count, t), 0)
    out = jnp.zeros((count, t), F32)
    cur = s
    for r in range(count):
        m = jnp.max(cur, axis=0, keepdims=True)
        out = jnp.where(ridx == r, m, out)
        if r + 1 < count:
            cur = jnp.where(cur == m, -jnp.inf, cur)
    return out


def _peer_route_kernel(x_ref, wq_ref, keys_ref, at_ref, bt_ref, e1_ref, e2_ref, tau_ref):
    q = jnp.dot(x_ref[...], wq_ref[...], preferred_element_type=F32).astype(BF16)
    tn = q.shape[0]
    limits = _peer_candidate_plan()
    row8 = lax.broadcasted_iota(jnp.int32, (8, tn), 0)
    for h in range(PEER_HEADS):
        st = []
        for half in range(2):
            col = (h * 2 + half) * PEER_NKEYS
            s = lax.dot_general(keys_ref[h, half], q[:, col:col + PEER_NKEYS],
                                (((1,), (1,)), ((), ())), preferred_element_type=F32)
            st.append(s * LOG2E)
        v1 = _top_rows(st[0], PEER_TOPK)
        v2 = _top_rows(st[1], PEER_TOPK)
        groups = [v1[0:1, :] + v2[0:8, :], v1[0:1, :] + v2[8:16, :]]
        for a in range(1, 8):
            cand = v1[a:a + 1, :] + v2[0:8, :]
            if limits[a] < 8:
                cand = jnp.where(row8 < limits[a], cand, -jnp.inf)
            groups.append(cand)
        groups.append(v1[8:16, :] + v2[0:1, :])
        cand = jnp.concatenate(groups, axis=0)
        top = _top_rows(cand, PEER_TOPK)
        z = jnp.sum(jnp.exp2(top - top[0:1, :]), axis=0, keepdims=True)
        at_ref[h] = st[0]
        bt_ref[h] = st[1]
        e1_ref[h] = jnp.exp2(st[0] - v1[0:1, :])
        e2_ref[h] = jnp.exp2(st[1] - v2[0:1, :]) / z
        tau_ref[h] = top[PEER_TOPK - 1:PEER_TOPK, :]


def _peer_route(xb, wq, keys):
    n, d = xb.shape
    tn = _pick(n, (256, 128))
    qw = wq.shape[1]
    tab = lambda: pl.BlockSpec((PEER_HEADS, PEER_NKEYS, tn), lambda i: (0, 0, i))
    tab_shape = jax.ShapeDtypeStruct((PEER_HEADS, PEER_NKEYS, n), F32)
    return pl.pallas_call(
        _peer_route_kernel,
        grid=(n // tn,),
        in_specs=[pl.BlockSpec((tn, d), lambda i: (i, 0)),
                  _resident((d, qw), lambda i: (0, 0)),
                  _resident(keys.shape, lambda i: (0, 0, 0, 0))],
        out_specs=[tab(), tab(), tab(), tab(),
                   pl.BlockSpec((PEER_HEADS, 1, tn), lambda i: (0, 0, i))],
        out_shape=[tab_shape, tab_shape, tab_shape, tab_shape,
                   jax.ShapeDtypeStruct((PEER_HEADS, 1, n), F32)],
        compiler_params=_params("parallel"),
        name="peer_route",
    )(xb, wq, keys)


def _gelu_tanh(x):
    return 0.5 * x * (1.0 + jnp.tanh(math.sqrt(2.0 / math.pi) * (x + 0.044715 * (x * x * x))))


PEER_HALF = 512
PEER_STEP = 2 * PEER_HALF

def _peer_mix_kernel(x_ref, at_ref, bt_ref, e1_ref, e2_ref, tau_ref, u_ref, vt_ref, g_ref, b_ref,
                     o_ref, xb_ref, acc_ref, ht_ref, act_ref, *, alpha):
    eb = pl.program_id(1)
    tn = x_ref.shape[0]
    groups = PEER_HALF // PEER_NKEYS

    @pl.when(eb == 0)
    def _():
        xb_ref[...] = x_ref[...].T.astype(BF16)
        acc_ref[...] = jnp.zeros(acc_ref.shape, F32)

    for hf in range(2):
        ht_ref[hf] = jnp.dot(u_ref[hf * PEER_HALF:(hf + 1) * PEER_HALF, :], xb_ref[...],
                             preferred_element_type=F32)

    row8 = pl.multiple_of(eb * 8, 8)
    for hf in range(2):
        for tc in range(tn // LANES):
            cs = slice(tc * LANES, (tc + 1) * LANES)
            a_grp = [at_ref[h, pl.ds(row8, 8), cs] for h in range(PEER_HEADS)]
            e_grp = [e1_ref[h, pl.ds(row8, 8), cs] for h in range(PEER_HEADS)]
            for ii in range(groups):
                r = hf * groups + ii
                gsum = jnp.zeros((PEER_NKEYS, LANES), F32)
                for h in range(PEER_HEADS):
                    total = a_grp[h][r:r + 1] + bt_ref[h, :, cs]
                    gate = e_grp[h][r:r + 1] * e2_ref[h, :, cs]
                    gsum = gsum + jnp.where(total >= tau_ref[h, :, cs], gate, 0.0)
                rs = slice(ii * PEER_NKEYS, (ii + 1) * PEER_NKEYS)
                act_ref[hf, rs, cs] = (_gelu_tanh(ht_ref[hf, rs, cs]) * gsum).astype(BF16)
        acc_ref[...] += jnp.dot(vt_ref[:, hf * PEER_HALF:(hf + 1) * PEER_HALF], act_ref[hf],
                                preferred_element_type=F32)

    @pl.when(eb == pl.num_programs(1) - 1)
    def _():
        y = alpha * x_ref[...] + acc_ref[...].T
        o_ref[...] = _layer_norm(y, g_ref[...], b_ref[...])


def _peer_mix(x, tabs, u, vt, g, b, alpha):
    n, d = x.shape
    e = u.shape[0]
    assert e == PEER_NKEYS * PEER_NKEYS and e % PEER_STEP == 0
    at, bt, e1, e2, tau = tabs
    tn = _pick(n, (512, 256, 128))
    once = lambda shape, imap: pl.BlockSpec(shape, imap, pipeline_mode=pl.Buffered(1))
    tab = lambda: once((PEER_HEADS, PEER_NKEYS, tn), lambda t, j: (0, 0, t))
    vec = lambda: pl.BlockSpec((1, d), lambda t, j: (0, 0))
    kern = functools.partial(_peer_mix_kernel, alpha=alpha)
    return pl.pallas_call(
        kern,
        grid=(n // tn, e // PEER_STEP),
        in_specs=[once((tn, d), lambda t, j: (t, 0)),
                  tab(), tab(), tab(), tab(),
                  once((PEER_HEADS, 1, tn), lambda t, j: (0, 0, t)),
                  pl.BlockSpec((PEER_STEP, d), lambda t, j: (j, 0)),
                  pl.BlockSpec((d, PEER_STEP), lambda t, j: (0, j)),
                  vec(), vec()],
        out_specs=pl.BlockSpec((tn, d), lambda t, j: (t, 0)),
        out_shape=jax.ShapeDtypeStruct((n, d), F32),
        scratch_shapes=[pltpu.VMEM((d, tn), BF16), pltpu.VMEM((d, tn), F32),
                        pltpu.VMEM((2, PEER_HALF, tn), F32), pltpu.VMEM((2, PEER_HALF, tn), BF16)],
        compiler_params=_params("parallel", "arbitrary"),
        name="peer_mix",
    )(x, at, bt, e1, e2, tau, u, vt, g.reshape(1, d), b.reshape(1, d))


def _ple_kernel(xr_ref, xc_ref, pe_ref, wg_ref, wp_ref, o_ref):
    gate = _sigmoid(jnp.dot(xr_ref[...].astype(BF16), wg_ref[...], preferred_element_type=F32))
    emb = jnp.dot(pe_ref[...].astype(BF16), wp_ref[...], preferred_element_type=F32)
    o_ref[...] = xc_ref[...] + gate * emb


def _ple(x, pe, w_gate, w_ple):
    n, d = x.shape
    pd = pe.shape[1]
    tm = _pick(n, (512, 256, 128, 64, 32, 16, 8))
    tn = _pick(d, (1024, 512, 256, 128))
    return pl.pallas_call(
        _ple_kernel,
        grid=(d // tn, n // tm),
        in_specs=[pl.BlockSpec((tm, d), lambda j, i: (i, 0)),
                  pl.BlockSpec((tm, tn), lambda j, i: (i, j)),
                  pl.BlockSpec((tm, pd), lambda j, i: (i, 0)),
                  pl.BlockSpec((d, tn), lambda j, i: (0, j)),
                  pl.BlockSpec((pd, tn), lambda j, i: (0, j))],
        out_specs=pl.BlockSpec((tm, tn), lambda j, i: (i, j)),
        out_shape=jax.ShapeDtypeStruct((n, d), F32),
        compiler_params=_params("parallel", "parallel"),
        name="ple",
    )(x, x, pe, w_gate, w_ple)


def _rope_tables(pos, rows_per_block_hint):
    inv = ROPE_THETA ** (-jnp.arange(0, ROT_DIM, 2, dtype=F32) / ROT_DIM)
    ang = pos.astype(F32)[:, None] * inv[None, :]
    cos, sin = jnp.cos(ang), jnp.sin(ang)
    t = pos.shape[0]
    half = ROT_DIM // 2
    ones = jnp.ones((t, HEAD_DIM - ROT_DIM), F32)
    zeros_h = jnp.zeros((t, half), F32)
    zeros_r = jnp.zeros((t, HEAD_DIM - ROT_DIM), F32)
    c = jnp.concatenate([cos, cos, ones], axis=1)
    sa = jnp.concatenate([-sin, zeros_h, zeros_r], axis=1)
    sb = jnp.concatenate([zeros_h, sin, zeros_r], axis=1)
    if t < rows_per_block_hint:
        reps = rows_per_block_hint // t
        c, sa, sb = (jnp.tile(a, (reps, 1)) for a in (c, sa, sb))
    return c, sa, sb


def _layer(x, pe, pos, k_past, v_past, conv_past, layer_idx, depth, wts):
    (w_in, lam_q, lam_k, subln_g, w_att_out, w_dw, b_dw, cln_g, cln_b, w_conv_out, w_o,
     ln1_g, ln1_b, peer_wq, peer_keys, peer_u, peer_v, ln2_g, ln2_b, w_ple, w_ple_gate) = wts
    b, t, d = x.shape
    n = b * t
    att_w = N_HEADS * HEAD_W
    conv_dim = w_dw.shape[1]
    alpha = (2 * depth) ** 0.25
    lam_init = 0.8 - 0.6 * math.exp(-0.3 * layer_idx)

    x2d = x.reshape(n, d)
    xb = x2d.astype(BF16)
    tm, _ = _proj_tiles(n, att_w)
    tabs = _rope_tables(pos, tm)

    (qb,) = _rope_proj(xb, w_in, 0, att_w, tabs, (BF16,))
    c = _glu_proj(xb, w_in, 3 * att_w, conv_dim)
    gates = _sigmoid_proj(xb, w_in, 3 * att_w + 2 * conv_dim, 2 * d)

    shp = (b, t, att_w)
    heads = (b, t, N_HEADS, HEAD_W)
    if k_past is None:
        k_f32, kb = _rope_proj(xb, w_in, att_w, att_w, tabs, (F32, BF16))
        v_f32, vb = _plain_proj(xb, w_in, 2 * att_w, att_w, (F32, BF16))
        o_n = _attn_prompt(qb.reshape(shp), kb.reshape(shp), vb.reshape(shp),
                           lam_q, lam_k, subln_g, lam_init)
    else:
        (k_f32,) = _rope_proj(xb, w_in, att_w, att_w, tabs, (F32,))
        (v_f32,) = _plain_proj(xb, w_in, 2 * att_w, att_w, (F32,))
        o_n = _attn_sample(qb.reshape(shp), k_f32.reshape(heads), v_f32.reshape(heads),
                           k_past, v_past, layer_idx, lam_q, lam_k, subln_g, lam_init)

    c3 = c.reshape(b, t, conv_dim)
    past32 = jnp.pad(conv_past, ((0, 0), (CONV_HALO - (CONV_W - 1), 0), (0, 0)))
    w_dw32 = jnp.pad(w_dw, ((0, CONV_HALO - CONV_W), (0, 0)))
    cact = _conv_branch(c3, past32, w_dw32, b_dw, cln_g, cln_b)
    conv_state = jnp.concatenate([conv_past, c3], axis=1)[:, -(CONV_W - 1):]

    mix = _branch_mix(o_n.reshape(n, att_w), cact.reshape(n, conv_dim), gates, w_att_out, w_conv_out)
    x1, x1b = _out_ln(mix, x2d, w_o, ln1_g, ln1_b, alpha)

    route = _peer_route(x1b, peer_wq, peer_keys)
    x2 = _peer_mix(x1, route, peer_u, peer_v, ln2_g, ln2_b, alpha)

    y = _ple(x2, pe.reshape(n, pe.shape[-1]), w_ple_gate, w_ple)
    return (y.reshape(b, t, d), k_f32.reshape(b, t, N_HEADS, HEAD_W),
            v_f32.reshape(b, t, N_HEADS, HEAD_W), conv_state)


def kernel(x_prompt, x_sample, cache_k, cache_v, state_conv, p_prompt, p_sample, w_in, lam_q, lam_k, subln_g, w_att_out, w_dw, b_dw, cln_g, cln_b, w_conv_out, w_o, ln1_g, ln1_b, peer_wq, peer_keys, peer_u, peer_v, ln2_g, ln2_b, w_ple, w_ple_gate):
    depth = w_in.shape[0]
    xp, xs = x_prompt, x_sample
    pos_p = jnp.arange(x_prompt.shape[1])
    pos_s = cache_k.shape[2] + jnp.arange(x_sample.shape[1])
    mats = (w_in, w_att_out, w_conv_out, w_o, peer_wq, peer_keys, peer_u, peer_v, w_ple, w_ple_gate)
    w_in_b, w_att_b, w_conv_b, w_o_b, wq_b, keys_b, u_b, v_b, w_ple_b, w_pg_b = (
        m.astype(BF16) for m in mats)
    v_b = jnp.swapaxes(v_b, 1, 2)
    outs = [[] for _ in range(6)]
    for l in range(depth):
        wts = (w_in_b[l], lam_q[l], lam_k[l], subln_g[l], w_att_b[l], w_dw[l], b_dw[l],
               cln_g[l], cln_b[l], w_conv_b[l], w_o_b[l], ln1_g[l], ln1_b[l],
               wq_b[l], keys_b[l], u_b[l], v_b[l], ln2_g[l], ln2_b[l], w_ple_b[l], w_pg_b[l])
        zero_conv = jnp.zeros((xp.shape[0], CONV_W - 1, w_dw.shape[2]), xp.dtype)
        xp, kp, vp, cp = _layer(xp, p_prompt[l], pos_p, None, None, zero_conv, l, depth, wts)
        xs, ks, vs, cs = _layer(xs, p_sample[l], pos_s, cache_k, cache_v, state_conv[l],
                                l, depth, wts)
        for lst, val in zip(outs, (kp, vp, cp, ks, vs, cs)):
            lst.append(val)
    return (xp, xs) + tuple(jnp.stack(o) for o in outs)
```

```python
import functools
import math

import jax
import jax.numpy as jnp
from jax import lax
from jax.experimental import pallas as pl
from jax.experimental.pallas import tpu as pltpu

F32 = jnp.float32
BF16 = jnp.bfloat16

CHUNK = 64
N_HEADS = 8
HEAD_DIM = 128
HEAD_W = 2 * HEAD_DIM
ROT_DIM = HEAD_DIM // 4
ROPE_THETA = 500000.0
CONV_W = 31
CONV_HALO = 32
PEER_HEADS = 8
PEER_NKEYS = 128
PEER_TOPK = 16
LN_EPS = 1e-5
LOG2E = math.log2(math.e)
NEG_BIG = -1e30
LANES = 128
SUBLANES = 8
VMEM_LIMIT = 56 * 2**20


def _pick(n, prefs):
    for p in prefs:
        if n % p == 0:
            return p
    return n


def _params(*sem):
    return pltpu.CompilerParams(dimension_semantics=sem, vmem_limit_bytes=VMEM_LIMIT)


def _resident(shape, index_map):
    return pl.BlockSpec(shape, index_map, pipeline_mode=pl.Buffered(1))


def _layer_norm(x, g, b):
    mu = jnp.mean(x, axis=-1, keepdims=True)
    d = x - mu
    var = jnp.mean(d * d, axis=-1, keepdims=True)
    return d * lax.rsqrt(var + LN_EPS) * g + b


def _sigmoid(x):
    return 1.0 / (1.0 + jnp.exp(-x))


def _rope_proj_kernel(x_ref, w_ref, cos_ref, sa_ref, sb_ref, *out_refs):
    acc = jnp.dot(x_ref[...], w_ref[...], preferred_element_type=F32)
    width = acc.shape[1]
    reps = width // LANES
    cos = jnp.concatenate([cos_ref[...]] * reps, axis=1)
    sa = jnp.concatenate([sa_ref[...]] * reps, axis=1)
    sb = jnp.concatenate([sb_ref[...]] * reps, axis=1)
    half = ROT_DIM // 2
    y = (acc * cos + pltpu.roll(acc, width - half, 1) * sa
         + pltpu.roll(acc, half, 1) * sb)
    for o in out_refs:
        o[...] = y.astype(o.dtype)


def _plain_proj_kernel(x_ref, w_ref, *out_refs):
    acc = jnp.dot(x_ref[...], w_ref[...], preferred_element_type=F32)
    for o in out_refs:
        o[...] = acc.astype(o.dtype)


def _glu_proj_kernel(x_ref, wa_ref, wg_ref, o_ref):
    x = x_ref[...]
    a = jnp.dot(x, wa_ref[...], preferred_element_type=F32)
    g = jnp.dot(x, wg_ref[...], preferred_element_type=F32)
    o_ref[...] = a * _sigmoid(g)


def _sigmoid_proj_kernel(x_ref, w_ref, o_ref):
    acc = jnp.dot(x_ref[...], w_ref[...], preferred_element_type=F32)
    o_ref[...] = _sigmoid(acc)


def _proj_tiles(n, width):
    tm = _pick(n, (1024, 512, 256, 128, 64, 32, 16, 8))
    tn = _pick(width, (1024, 512, 256, 128))
    return tm, tn


def _rope_proj(xb, w, col0, width, tabs, out_dtypes):
    n, d = xb.shape
    tm, tn = _proj_tiles(n, width)
    tab_blocks = tabs[0].shape[0] // tm
    grid = (width // tn, n // tm)
    tab_spec = pl.BlockSpec((tm, LANES), lambda j, i: (i % tab_blocks, 0))
    return pl.pallas_call(
        _rope_proj_kernel,
        grid=grid,
        in_specs=[pl.BlockSpec((tm, d), lambda j, i: (i, 0)),
                  pl.BlockSpec((d, tn), lambda j, i: (0, col0 // tn + j)),
                  tab_spec, tab_spec, tab_spec],
        out_specs=[pl.BlockSpec((tm, tn), lambda j, i: (i, j)) for _ in out_dtypes],
        out_shape=[jax.ShapeDtypeStruct((n, width), dt) for dt in out_dtypes],
        compiler_params=_params("parallel", "parallel"),
        name="rope_proj",
    )(xb, w, *tabs)


def _plain_proj(xb, w, col0, width, out_dtypes):
    n, d = xb.shape
    tm, tn = _proj_tiles(n, width)
    return pl.pallas_call(
        _plain_proj_kernel,
        grid=(width // tn, n // tm),
        in_specs=[pl.BlockSpec((tm, d), lambda j, i: (i, 0)),
                  pl.BlockSpec((d, tn), lambda j, i: (0, col0 // tn + j))],
        out_specs=[pl.BlockSpec((tm, tn), lambda j, i: (i, j)) for _ in out_dtypes],
        out_shape=[jax.ShapeDtypeStruct((n, width), dt) for dt in out_dtypes],
        compiler_params=_params("parallel", "parallel"),
        name="plain_proj",
    )(xb, w)


def _glu_proj(xb, w, col0, width):
    n, d = xb.shape
    tm, tn = _proj_tiles(n, width)
    return pl.pallas_call(
        _glu_proj_kernel,
        grid=(width // tn, n // tm),
        in_specs=[pl.BlockSpec((tm, d), lambda j, i: (i, 0)),
                  pl.BlockSpec((d, tn), lambda j, i: (0, col0 // tn + j)),
                  pl.BlockSpec((d, tn), lambda j, i: (0, (col0 + width) // tn + j))],
        out_specs=pl.BlockSpec((tm, tn), lambda j, i: (i, j)),
        out_shape=jax.ShapeDtypeStruct((n, width), F32),
        compiler_params=_params("parallel", "parallel"),
        name="glu_proj",
    )(xb, w, w)


def _sigmoid_proj(xb, w, col0, width):
    n, d = xb.shape
    tm, tn = _proj_tiles(n, width)
    return pl.pallas_call(
        _sigmoid_proj_kernel,
        grid=(width // tn, n // tm),
        in_specs=[pl.BlockSpec((tm, d), lambda j, i: (i, 0)),
                  pl.BlockSpec((d, tn), lambda j, i: (0, col0 // tn + j))],
        out_specs=pl.BlockSpec((tm, tn), lambda j, i: (i, j)),
        out_shape=jax.ShapeDtypeStruct((n, width), F32),
        compiler_params=_params("parallel", "parallel"),
        name="gate_proj",
    )(xb, w)


def _lambda_value(lq_ref, lk_ref, lam_init):
    lf = lq_ref[...] * lk_ref[...]
    e = jnp.exp(jnp.sum(lf, axis=1, keepdims=True))
    return e[0:1, :] - e[1:2, :] + lam_init


def _finish_heads(o1, o2, lam, g_ref, lam_init):
    o = o1 - lam * o2
    ms = jnp.mean(o * o, axis=-1, keepdims=True)
    return o * lax.rsqrt(ms + LN_EPS) * g_ref[...] * (1.0 - lam_init)


ATTN_HEADS_PER_STEP = 2


def _attn_prompt_kernel(lq_ref, lk_ref, g_ref, q_ref, k_ref, v_ref, o_ref,
                        m_ref, l_ref, acc_ref, *, tq, lam_init):
    i = pl.program_id(2)
    c = (HEAD_DIM ** -0.5) * LOG2E
    chains = 2 * ATTN_HEADS_PER_STEP
    m_ref[...] = jnp.full(m_ref.shape, NEG_BIG, F32)
    l_ref[...] = jnp.zeros(l_ref.shape, F32)
    acc_ref[...] = jnp.zeros(acc_ref.shape, F32)

    def step(kb, mask):
        r0 = pl.multiple_of(kb * tq, tq)
        for ch in range(chains):
            hh, comp = divmod(ch, 2)
            qk_cols = slice(hh * HEAD_W + comp * HEAD_DIM, hh * HEAD_W + (comp + 1) * HEAD_DIM)
            s = lax.dot_general(q_ref[:, qk_cols], k_ref[pl.ds(r0, tq), qk_cols],
                                (((1,), (1,)), ((), ())), preferred_element_type=F32)
            if mask is not None:
                s = jnp.where(mask, s, NEG_BIG)
            m_prev = m_ref[ch]
            m_new = jnp.maximum(m_prev, jnp.max(s, axis=1, keepdims=True))
            alpha = jnp.exp2((m_prev - m_new) * c)
            p = jnp.exp2((s - jnp.concatenate([m_new] * (tq // LANES), axis=1)) * c)
            l_ref[ch] = alpha * l_ref[ch] + jnp.sum(p, axis=1, keepdims=True)
            v = v_ref[pl.ds(r0, tq), hh * HEAD_W:(hh + 1) * HEAD_W]
            acc_ref[ch] = (acc_ref[ch] * jnp.concatenate([alpha] * (HEAD_W // LANES), axis=1)
                           + jnp.dot(p.astype(BF16), v, preferred_element_type=F32))
            m_ref[ch] = m_new

    def body(kb, carry):
        step(kb, None)
        return carry

    lax.fori_loop(0, i, body, 0)
    qc = lax.broadcasted_iota(jnp.int32, (tq, tq), 0) // CHUNK
    kc = lax.broadcasted_iota(jnp.int32, (tq, tq), 1) // CHUNK
    step(i, kc <= qc)

    lam = _lambda_value(lq_ref, lk_ref, lam_init)
    rep = HEAD_W // LANES
    for hh in range(ATTN_HEADS_PER_STEP):
        o1 = acc_ref[2 * hh] / jnp.concatenate([l_ref[2 * hh]] * rep, axis=1)
        o2 = acc_ref[2 * hh + 1] / jnp.concatenate([l_ref[2 * hh + 1]] * rep, axis=1)
        o_ref[:, hh * HEAD_W:(hh + 1) * HEAD_W] = _finish_heads(
            o1, o2, lam, g_ref, lam_init).astype(o_ref.dtype)


def _attn_prompt(qb, kb, vb, lam_q, lam_k, subln_g, lam_init):
    b, s, w = qb.shape
    tq = _pick(s, (512, 256, 128, 64))
    hw = ATTN_HEADS_PER_STEP * HEAD_W
    chains = 2 * ATTN_HEADS_PER_STEP
    kern = functools.partial(_attn_prompt_kernel, tq=tq, lam_init=lam_init)
    small = lambda shape: pl.BlockSpec(shape, lambda bi, h, i: (0, 0))
    seq = lambda: pl.BlockSpec((None, s, hw), lambda bi, h, i: (bi, 0, h), pipeline_mode=pl.Buffered(1))
    return pl.pallas_call(
        kern,
        grid=(b, N_HEADS // ATTN_HEADS_PER_STEP, s // tq),
        in_specs=[small((2, HEAD_DIM)), small((2, HEAD_DIM)), small((1, HEAD_W)),
                  pl.BlockSpec((None, tq, hw), lambda bi, h, i: (bi, i, h)),
                  seq(), seq()],
        out_specs=pl.BlockSpec((None, tq, hw), lambda bi, h, i: (bi, i, h)),
        out_shape=jax.ShapeDtypeStruct((b, s, w), BF16),
        scratch_shapes=[pltpu.VMEM((chains, tq, LANES), F32), pltpu.VMEM((chains, tq, LANES), F32),
                        pltpu.VMEM((chains, tq, HEAD_W), F32)],
        compiler_params=_params("parallel", "parallel", "arbitrary"),
        name="attn_prompt",
    )(lam_q, lam_k, subln_g.reshape(1, HEAD_W), qb, kb, vb)


def _attn_sample_kernel(lq_ref, lk_ref, g_ref, q_ref, kn_ref, vn_ref, kp_ref, vp_ref, o_ref,
                        qbd_ref, bias_ref, m_ref, l_ref, acc_ref, *, lam_init, t):
    j = pl.program_id(1)
    c = (HEAD_DIM ** -0.5) * LOG2E
    rows = N_HEADS * 2 * t

    def attend(k_flat, v_flat):
        n = k_flat.shape[0]
        s = lax.dot_general(qbd_ref[...], k_flat, (((1,), (1,)), ((), ())),
                            preferred_element_type=F32)
        s = s + jnp.concatenate([bias_ref[...]] * (n // LANES), axis=1)
        m_prev = m_ref[...]
        m_new = jnp.maximum(m_prev, jnp.max(s, axis=1, keepdims=True))
        alpha = jnp.exp2((m_prev - m_new) * c)
        p = jnp.exp2((s - jnp.concatenate([m_new] * (n // LANES), axis=1)) * c)
        l_ref[...] = alpha * l_ref[...] + jnp.sum(p, axis=1, keepdims=True)
        acc_ref[...] = (acc_ref[...] * jnp.concatenate([alpha] * (HEAD_W // LANES), axis=1)
                        + jnp.dot(p.astype(BF16), v_flat, preferred_element_type=F32))
        m_ref[...] = m_new

    @pl.when(j == 0)
    def _():
        qbd_ref[...] = jnp.zeros(qbd_ref.shape, BF16)
        for h in range(N_HEADS):
            for comp in range(2):
                r0 = (h * 2 + comp) * t
                cols = slice(comp * HEAD_DIM, (comp + 1) * HEAD_DIM)
                qbd_ref[r0:r0 + t, cols] = q_ref[:, h * HEAD_W + comp * HEAD_DIM:
                                                 h * HEAD_W + (comp + 1) * HEAD_DIM]
        row_head = lax.broadcasted_iota(jnp.int32, (rows, LANES), 0) // (2 * t)
        key_head = lax.broadcasted_iota(jnp.int32, (rows, LANES), 1) % N_HEADS
        bias_ref[...] = jnp.where(row_head == key_head, 0.0, NEG_BIG)
        m_ref[...] = jnp.full(m_ref.shape, NEG_BIG, F32)
        l_ref[...] = jnp.zeros(l_ref.shape, F32)
        acc_ref[...] = jnp.zeros(acc_ref.shape, F32)
        attend(kn_ref[...].reshape(t * N_HEADS, HEAD_W).astype(BF16),
               vn_ref[...].reshape(t * N_HEADS, HEAD_W).astype(BF16))

    pb = kp_ref.shape[0]
    attend(kp_ref[...].reshape(pb * N_HEADS, HEAD_W).astype(BF16),
           vp_ref[...].reshape(pb * N_HEADS, HEAD_W).astype(BF16))

    @pl.when(j == pl.num_programs(1) - 1)
    def _():
        lam = _lambda_value(lq_ref, lk_ref, lam_init)
        rep = HEAD_W // LANES
        o = acc_ref[...] / jnp.concatenate([l_ref[...]] * rep, axis=1)
        for h in range(N_HEADS):
            r0 = h * 2 * t
            o_ref[:, h * HEAD_W:(h + 1) * HEAD_W] = _finish_heads(
                o[r0:r0 + t], o[r0 + t:r0 + 2 * t], lam, g_ref, lam_init).astype(o_ref.dtype)


def _attn_sample(qb, k_new, v_new, k_cache, v_cache, layer_idx, lam_q, lam_k, subln_g, lam_init):
    b, t, w = qb.shape
    p = k_cache.shape[2]
    pb = _pick(p, (512, 256, 128, 64, 32, 16))
    rows = N_HEADS * 2 * t
    kern = functools.partial(_attn_sample_kernel, lam_init=lam_init, t=t)
    small = lambda shape: pl.BlockSpec(shape, lambda bi, j: (0, 0))
    new = pl.BlockSpec((None, t, N_HEADS, HEAD_W), lambda bi, j: (bi, 0, 0, 0))
    past = pl.BlockSpec((None, None, pb, N_HEADS, HEAD_W), lambda bi, j: (layer_idx, bi, j, 0, 0))
    return pl.pallas_call(
        kern,
        grid=(b, p // pb),
        in_specs=[small((2, HEAD_DIM)), small((2, HEAD_DIM)), small((1, HEAD_W)),
                  pl.BlockSpec((None, t, w), lambda bi, j: (bi, 0, 0)),
                  new, new, past, past],
        out_specs=pl.BlockSpec((None, t, w), lambda bi, j: (bi, 0, 0)),
        out_shape=jax.ShapeDtypeStruct((b, t, w), BF16),
        scratch_shapes=[pltpu.VMEM((rows, HEAD_W), BF16), pltpu.VMEM((rows, LANES), F32),
                        pltpu.VMEM((rows, LANES), F32), pltpu.VMEM((rows, LANES), F32),
                        pltpu.VMEM((rows, HEAD_W), F32)],
        compiler_params=_params("parallel", "arbitrary"),
        name="attn_sample",
    )(lam_q, lam_k, subln_g.reshape(1, HEAD_W), qb, k_new, v_new, k_cache, v_cache)


def _conv_kernel(*refs, tm, has_halo):
    if has_halo:
        (past_ref, halo_ref, cur_ref, w_ref, b_ref, g_ref, beta_ref, o_ref,
         win_ref, sh_ref, y_ref) = refs
    else:
        past_ref, cur_ref, w_ref, b_ref, g_ref, beta_ref, o_ref, win_ref, sh_ref, y_ref = refs
        halo_ref = None
    i = pl.program_id(1)
    if has_halo:
        @pl.when(i == 0)
        def _():
            win_ref[0:CONV_HALO, :] = past_ref[...]

        @pl.when(i > 0)
        def _():
            win_ref[0:CONV_HALO, :] = halo_ref[...]
    else:
        win_ref[0:CONV_HALO, :] = past_ref[...]
    win_ref[CONV_HALO:, :] = cur_ref[...]

    width = cur_ref.shape[1]
    rc = _pick(tm, (32, 16, 8))
    lead = CONV_HALO - (CONV_W - 1)
    sh_rows = sh_ref.shape[1]

    def col_body(cj, carry):
        c0 = pl.multiple_of(cj * LANES, LANES)
        wk = [jnp.broadcast_to(w_ref[pl.ds(k, 1), pl.ds(c0, LANES)], (rc, LANES))
              for k in range(CONV_W)]
        bias = jnp.broadcast_to(b_ref[:, pl.ds(c0, LANES)], (rc, LANES))
        for s in range(1, SUBLANES):
            sh_ref[s - 1] = win_ref[pl.ds(s, sh_rows), pl.ds(c0, LANES)]
        for r in range(tm // rc):
            acc = bias
            for k in range(CONV_W):
                q, s = divmod(lead + k, SUBLANES)
                row = r * rc + q * SUBLANES
                if s == 0:
                    tap = win_ref[pl.ds(row, rc), pl.ds(c0, LANES)]
                else:
                    tap = sh_ref[s - 1, pl.ds(row, rc), :]
                acc = acc + tap * wk[k]
            y_ref[pl.ds(r * rc, rc), pl.ds(c0, LANES)] = acc
        return carry

    lax.fori_loop(0, width // LANES, col_body, 0)
    y = _layer_norm(y_ref[...], g_ref[...], beta_ref[...])
    o_ref[...] = (y * _sigmoid(y)).astype(o_ref.dtype)


def _conv_branch(c, past32, w_dw32, b_dw, cln_g, cln_b):
    b, t, ch = c.shape
    tm = _pick(t, (256, 128, 64, 32, 16, 8))
    has_halo = t > tm
    kern = functools.partial(_conv_kernel, tm=tm, has_halo=has_halo)
    vec = lambda: pl.BlockSpec((1, ch), lambda bi, i: (0, 0))
    hb = tm // CONV_HALO
    in_specs = [pl.BlockSpec((None, CONV_HALO, ch), lambda bi, i: (bi, 0, 0))]
    args = [past32]
    if has_halo:
        in_specs.append(pl.BlockSpec((None, CONV_HALO, ch),
                                     lambda bi, i: (bi, jnp.maximum(i * hb - 1, 0), 0)))
        args.append(c)
    in_specs += [pl.BlockSpec((None, tm, ch), lambda bi, i: (bi, i, 0)),
                 pl.BlockSpec((CONV_HALO, ch), lambda bi, i: (0, 0)),
                 vec(), vec(), vec()]
    args += [c, w_dw32, b_dw.reshape(1, ch), cln_g.reshape(1, ch), cln_b.reshape(1, ch)]
    return pl.pallas_call(
        kern,
        grid=(b, t // tm),
        in_specs=in_specs,
        out_specs=pl.BlockSpec((None, tm, ch), lambda bi, i: (bi, i, 0)),
        out_shape=jax.ShapeDtypeStruct((b, t, ch), BF16),
        scratch_shapes=[pltpu.VMEM((tm + CONV_HALO, ch), F32),
                        pltpu.VMEM((SUBLANES - 1, tm + CONV_HALO - SUBLANES, LANES), F32),
                        pltpu.VMEM((tm, ch), F32)],
        compiler_params=_params("parallel", "parallel"),
        name="conv_branch",
    )(*args)


def _mix_kernel(on_ref, ca_ref, ga_ref, gc_ref, wa_ref, wc_ref, o_ref):
    att = jnp.dot(on_ref[...], wa_ref[...], preferred_element_type=F32)
    cv = jnp.dot(ca_ref[...], wc_ref[...], preferred_element_type=F32)
    o_ref[...] = (ga_ref[...] * att + gc_ref[...] * cv).astype(o_ref.dtype)


def _branch_mix(o_n, cact, gates, w_att, w_conv):
    n, d = o_n.shape
    dm = w_att.shape[1]
    tm = _pick(n, (512, 256, 128, 64, 32, 16, 8))
    tn = _pick(dm, (1024, 512, 256, 128))
    nj = dm // tn
    return pl.pallas_call(
        _mix_kernel,
        grid=(nj, n // tm),
        in_specs=[pl.BlockSpec((tm, d), lambda j, i: (i, 0)),
                  pl.BlockSpec((tm, cact.shape[1]), lambda j, i: (i, 0)),
                  pl.BlockSpec((tm, tn), lambda j, i: (i, j)),
                  pl.BlockSpec((tm, tn), lambda j, i: (i, nj + j)),
                  pl.BlockSpec((d, tn), lambda j, i: (0, j)),
                  pl.BlockSpec((cact.shape[1], tn), lambda j, i: (0, j))],
        out_specs=pl.BlockSpec((tm, tn), lambda j, i: (i, j)),
        out_shape=jax.ShapeDtypeStruct((n, dm), BF16),
        compiler_params=_params("parallel", "parallel"),
        name="branch_mix",
    )(o_n, cact, gates, gates, w_att, w_conv)


def _out_ln_kernel(mix_ref, x_ref, w_ref, g_ref, b_ref, o_ref, ob_ref, *, alpha):
    y = alpha * x_ref[...] + jnp.dot(mix_ref[...], w_ref[...], preferred_element_type=F32)
    y = _layer_norm(y, g_ref[...], b_ref[...])
    o_ref[...] = y
    ob_ref[...] = y.astype(BF16)


def _out_ln(mix, x, w_o, g, b, alpha):
    n, d = x.shape
    tm = _pick(n, (512, 256, 128, 64, 32, 16, 8))
    vec = lambda: pl.BlockSpec((1, d), lambda i: (0, 0))
    return pl.pallas_call(
        functools.partial(_out_ln_kernel, alpha=alpha),
        grid=(n // tm,),
        in_specs=[pl.BlockSpec((tm, d), lambda i: (i, 0)),
                  pl.BlockSpec((tm, d), lambda i: (i, 0)),
                  _resident((d, d), lambda i: (0, 0)),
                  vec(), vec()],
        out_specs=[pl.BlockSpec((tm, d), lambda i: (i, 0)),
                   pl.BlockSpec((tm, d), lambda i: (i, 0))],
        out_shape=[jax.ShapeDtypeStruct((n, d), F32), jax.ShapeDtypeStruct((n, d), BF16)],
        compiler_params=_params("parallel"),
        name="out_ln1",
    )(mix, x, w_o, g.reshape(1, d), b.reshape(1, d))


def _peer_candidate_plan():
    return [PEER_TOPK // (a + 1) for a in range(PEER_TOPK)]


def _top_rows(s, count):
    rows, t = s.shape
    ridx = lax.broadcasted_iota(jnp.int32, (count, t), 0)
    out = jnp.zeros((count, t), F32)
    cur = s
    for r in range(count):
        m = jnp.max(cur, axis=0, keepdims=True)
        out = jnp.where(ridx == r, m, out)
        if r + 1 < count:
            cur = jnp.where(cur == m, -jnp.inf, cur)
    return out


def _peer_route_kernel(x_ref, wq_ref, keys_ref, at_ref, bt_ref, e1_ref, e2_ref, tau_ref):
    q = jnp.dot(x_ref[...], wq_ref[...], preferred_element_type=F32).astype(BF16)
    tn = q.shape[0]
    limits = _peer_candidate_plan()
    row8 = lax.broadcasted_iota(jnp.int32, (8, tn), 0)
    for h in range(PEER_HEADS):
        st = []
        for half in range(2):
            col = (h * 2 + half) * PEER_NKEYS
            s = lax.dot_general(keys_ref[h, half], q[:, col:col + PEER_NKEYS],
                                (((1,), (1,)), ((), ())), preferred_element_type=F32)
            st.append(s * LOG2E)
        v1 = _top_rows(st[0], PEER_TOPK)
        v2 = _top_rows(st[1], PEER_TOPK)
        groups = [v1[0:1, :] + v2[0:8, :], v1[0:1, :] + v2[8:16, :]]
        for a in range(1, 8):
            cand = v1[a:a + 1, :] + v2[0:8, :]
            if limits[a] < 8:
                cand = jnp.where(row8 < limits[a], cand, -jnp.inf)
            groups.append(cand)
        groups.append(v1[8:16, :] + v2[0:1, :])
        cand = jnp.concatenate(groups, axis=0)
        top = _top_rows(cand, PEER_TOPK)
        z = jnp.sum(jnp.exp2(top - top[0:1, :]), axis=0, keepdims=True)
        at_ref[h] = st[0]
        bt_ref[h] = st[1]
        e1_ref[h] = jnp.exp2(st[0] - v1[0:1, :])
        e2_ref[h] = jnp.exp2(st[1] - v2[0:1, :]) / z
        tau_ref[h] = top[PEER_TOPK - 1:PEER_TOPK, :]


def _peer_route(xb, wq, keys):
    n, d = xb.shape
    tn = _pick(n, (256, 128))
    qw = wq.shape[1]
    tab = lambda: pl.BlockSpec((PEER_HEADS, PEER_NKEYS, tn), lambda i: (0, 0, i))
    tab_shape = jax.ShapeDtypeStruct((PEER_HEADS, PEER_NKEYS, n), F32)
    return pl.pallas_call(
        _peer_route_kernel,
        grid=(n // tn,),
        in_specs=[pl.BlockSpec((tn, d), lambda i: (i, 0)),
                  _resident((d, qw), lambda i: (0, 0)),
                  _resident(keys.shape, lambda i: (0, 0, 0, 0))],
        out_specs=[tab(), tab(), tab(), tab(),
                   pl.BlockSpec((PEER_HEADS, 1, tn), lambda i: (0, 0, i))],
        out_shape=[tab_shape, tab_shape, tab_shape, tab_shape,
                   jax.ShapeDtypeStruct((PEER_HEADS, 1, n), F32)],
        compiler_params=_params("parallel"),
        name="peer_route",
    )(xb, wq, keys)


def _gated_gelu(x, g):
    c = -2.0 * math.sqrt(2.0 / math.pi) * LOG2E
    w = x * ((x * x) * (0.044715 * c) + c)
    return (x * g) / (1.0 + jnp.exp2(w))


PEER_HALF = 512
PEER_STEP = 2 * PEER_HALF


def _peer_mix_kernel(x_ref, at_ref, bt_ref, e1_ref, e2_ref, tau_ref, u_ref, vt_ref, g_ref, b_ref,
                     o_ref, xb_ref, acc_ref, ht_ref, act_ref, *, alpha):
    eb = pl.program_id(1)
    tn = x_ref.shape[0]
    groups = PEER_HALF // PEER_NKEYS

    @pl.when(eb == 0)
    def _():
        xb_ref[...] = x_ref[...].T.astype(BF16)
        acc_ref[...] = jnp.zeros(acc_ref.shape, F32)

    for hf in range(2):
        ht_ref[hf] = jnp.dot(u_ref[hf * PEER_HALF:(hf + 1) * PEER_HALF, :], xb_ref[...],
                             preferred_element_type=F32)

    row8 = pl.multiple_of(eb * 8, 8)
    for hf in range(2):
        for tc in range(tn // LANES):
            cs = slice(tc * LANES, (tc + 1) * LANES)
            a_grp = [at_ref[h, pl.ds(row8, 8), cs] for h in range(PEER_HEADS)]
            e_grp = [e1_ref[h, pl.ds(row8, 8), cs] for h in range(PEER_HEADS)]
            for ii in range(groups):
                r = hf * groups + ii
                gsum = jnp.zeros((PEER_NKEYS, LANES), F32)
                for h in range(PEER_HEADS):
                    total = a_grp[h][r:r + 1] + bt_ref[h, :, cs]
                    gate = e_grp[h][r:r + 1] * e2_ref[h, :, cs]
                    gsum = gsum + jnp.where(total >= tau_ref[h, :, cs], gate, 0.0)
                rs = slice(ii * PEER_NKEYS, (ii + 1) * PEER_NKEYS)
                act_ref[hf, rs, cs] = _gated_gelu(ht_ref[hf, rs, cs], gsum).astype(BF16)
        acc_ref[...] += jnp.dot(vt_ref[:, hf * PEER_HALF:(hf + 1) * PEER_HALF], act_ref[hf],
                                preferred_element_type=F32)

    @pl.when(eb == pl.num_programs(1) - 1)
    def _():
        y = alpha * x_ref[...] + acc_ref[...].T
        o_ref[...] = _layer_norm(y, g_ref[...], b_ref[...])


def _peer_mix(x, tabs, u, vt, g, b, alpha):
    n, d = x.shape
    e = u.shape[0]
    assert e == PEER_NKEYS * PEER_NKEYS and e % PEER_STEP == 0
    at, bt, e1, e2, tau = tabs
    tn = _pick(n, (512, 256, 128))
    once = lambda shape, imap: pl.BlockSpec(shape, imap, pipeline_mode=pl.Buffered(1))
    tab = lambda: once((PEER_HEADS, PEER_NKEYS, tn), lambda t, j: (0, 0, t))
    vec = lambda: pl.BlockSpec((1, d), lambda t, j: (0, 0))
    kern = functools.partial(_peer_mix_kernel, alpha=alpha)
    return pl.pallas_call(
        kern,
        grid=(n // tn, e // PEER_STEP),
        in_specs=[once((tn, d), lambda t, j: (t, 0)),
                  tab(), tab(), tab(), tab(),
                  once((PEER_HEADS, 1, tn), lambda t, j: (0, 0, t)),
                  pl.BlockSpec((PEER_STEP, d), lambda t, j: (j, 0)),
                  pl.BlockSpec((d, PEER_STEP), lambda t, j: (0, j)),
                  vec(), vec()],
        out_specs=pl.BlockSpec((tn, d), lambda t, j: (t, 0)),
        out_shape=jax.ShapeDtypeStruct((n, d), F32),
        scratch_shapes=[pltpu.VMEM((d, tn), BF16), pltpu.VMEM((d, tn), F32),
                        pltpu.VMEM((2, PEER_HALF, tn), F32), pltpu.VMEM((2, PEER_HALF, tn), BF16)],
        compiler_params=_params("parallel", "arbitrary"),
        name="peer_mix",
    )(x, at, bt, e1, e2, tau, u, vt, g.reshape(1, d), b.reshape(1, d))


def _ple_kernel(xr_ref, xc_ref, pe_ref, wg_ref, wp_ref, o_ref):
    gate = _sigmoid(jnp.dot(xr_ref[...].astype(BF16), wg_ref[...], preferred_element_type=F32))
    emb = jnp.dot(pe_ref[...].astype(BF16), wp_ref[...], preferred_element_type=F32)
    o_ref[...] = xc_ref[...] + gate * emb


def _ple(x, pe, w_gate, w_ple):
    n, d = x.shape
    pd = pe.shape[1]
    tm = _pick(n, (512, 256, 128, 64, 32, 16, 8))
    tn = _pick(d, (1024, 512, 256, 128))
    return pl.pallas_call(
        _ple_kernel,
        grid=(d // tn, n // tm),
        in_specs=[pl.BlockSpec((tm, d), lambda j, i: (i, 0)),
                  pl.BlockSpec((tm, tn), lambda j, i: (i, j)),
                  pl.BlockSpec((tm, pd), lambda j, i: (i, 0)),
                  pl.BlockSpec((d, tn), lambda j, i: (0, j)),
                  pl.BlockSpec((pd, tn), lambda j, i: (0, j))],
        out_specs=pl.BlockSpec((tm, tn), lambda j, i: (i, j)),
        out_shape=jax.ShapeDtypeStruct((n, d), F32),
        compiler_params=_params("parallel", "parallel"),
        name="ple",
    )(x, x, pe, w_gate, w_ple)


def _rope_tables(pos, rows_per_block_hint):
    inv = ROPE_THETA ** (-jnp.arange(0, ROT_DIM, 2, dtype=F32) / ROT_DIM)
    ang = pos.astype(F32)[:, None] * inv[None, :]
    cos, sin = jnp.cos(ang), jnp.sin(ang)
    t = pos.shape[0]
    half = ROT_DIM // 2
    ones = jnp.ones((t, HEAD_DIM - ROT_DIM), F32)
    zeros_h = jnp.zeros((t, half), F32)
    zeros_r = jnp.zeros((t, HEAD_DIM - ROT_DIM), F32)
    c = jnp.concatenate([cos, cos, ones], axis=1)
    sa = jnp.concatenate([-sin, zeros_h, zeros_r], axis=1)
    sb = jnp.concatenate([zeros_h, sin, zeros_r], axis=1)
    if t < rows_per_block_hint:
        reps = rows_per_block_hint // t
        c, sa, sb = (jnp.tile(a, (reps, 1)) for a in (c, sa, sb))
    return c, sa, sb


def _layer(x, pe, pos, k_past, v_past, conv_past, layer_idx, depth, wts):
    (w_in, lam_q, lam_k, subln_g, w_att_out, w_dw, b_dw, cln_g, cln_b, w_conv_out, w_o,
     ln1_g, ln1_b, peer_wq, peer_keys, peer_u, peer_v, ln2_g, ln2_b, w_ple, w_ple_gate) = wts
    b, t, d = x.shape
    n = b * t
    att_w = N_HEADS * HEAD_W
    conv_dim = w_dw.shape[1]
    alpha = (2 * depth) ** 0.25
    lam_init = 0.8 - 0.6 * math.exp(-0.3 * layer_idx)

    x2d = x.reshape(n, d)
    xb = x2d.astype(BF16)
    tm, _ = _proj_tiles(n, att_w)
    tabs = _rope_tables(pos, tm)

    (qb,) = _rope_proj(xb, w_in, 0, att_w, tabs, (BF16,))
    c = _glu_proj(xb, w_in, 3 * att_w, conv_dim)
    gates = _sigmoid_proj(xb, w_in, 3 * att_w + 2 * conv_dim, 2 * d)

    shp = (b, t, att_w)
    heads = (b, t, N_HEADS, HEAD_W)
    if k_past is None:
        k_f32, kb = _rope_proj(xb, w_in, att_w, att_w, tabs, (F32, BF16))
        v_f32, vb = _plain_proj(xb, w_in, 2 * att_w, att_w, (F32, BF16))
        o_n = _attn_prompt(qb.reshape(shp), kb.reshape(shp), vb.reshape(shp),
                           lam_q, lam_k, subln_g, lam_init)
    else:
        (k_f32,) = _rope_proj(xb, w_in, att_w, att_w, tabs, (F32,))
        (v_f32,) = _plain_proj(xb, w_in, 2 * att_w, att_w, (F32,))
        o_n = _attn_sample(qb.reshape(shp), k_f32.reshape(heads), v_f32.reshape(heads),
                           k_past, v_past, layer_idx, lam_q, lam_k, subln_g, lam_init)

    c3 = c.reshape(b, t, conv_dim)
    past32 = jnp.pad(conv_past, ((0, 0), (CONV_HALO - (CONV_W - 1), 0), (0, 0)))
    w_dw32 = jnp.pad(w_dw, ((0, CONV_HALO - CONV_W), (0, 0)))
    cact = _conv_branch(c3, past32, w_dw32, b_dw, cln_g, cln_b)
    conv_state = jnp.concatenate([conv_past, c3], axis=1)[:, -(CONV_W - 1):]

    mix = _branch_mix(o_n.reshape(n, att_w), cact.reshape(n, conv_dim), gates, w_att_out, w_conv_out)
    x1, x1b = _out_ln(mix, x2d, w_o, ln1_g, ln1_b, alpha)

    route = _peer_route(x1b, peer_wq, peer_keys)
    x2 = _peer_mix(x1, route, peer_u, peer_v, ln2_g, ln2_b, alpha)

    y = _ple(x2, pe.reshape(n, pe.shape[-1]), w_ple_gate, w_ple)
    return (y.reshape(b, t, d), k_f32.reshape(b, t, N_HEADS, HEAD_W),
            v_f32.reshape(b, t, N_HEADS, HEAD_W), conv_state)


def kernel(x_prompt, x_sample, cache_k, cache_v, state_conv, p_prompt, p_sample, w_in, lam_q, lam_k, subln_g, w_att_out, w_dw, b_dw, cln_g, cln_b, w_conv_out, w_o, ln1_g, ln1_b, peer_wq, peer_keys, peer_u, peer_v, ln2_g, ln2_b, w_ple, w_ple_gate):
    depth = w_in.shape[0]
    xp, xs = x_prompt, x_sample
    pos_p = jnp.arange(x_prompt.shape[1])
    pos_s = cache_k.shape[2] + jnp.arange(x_sample.shape[1])
    mats = (w_in, w_att_out, w_conv_out, w_o, peer_wq, peer_keys, peer_u, peer_v, w_ple, w_ple_gate)
    w_in_b, w_att_b, w_conv_b, w_o_b, wq_b, keys_b, u_b, v_b, w_ple_b, w_pg_b = (
        m.astype(BF16) for m in mats)
    v_b = jnp.swapaxes(v_b, 1, 2)
    outs = [[] for _ in range(6)]
    for l in range(depth):
        wts = (w_in_b[l], lam_q[l], lam_k[l], subln_g[l], w_att_b[l], w_dw[l], b_dw[l],
               cln_g[l], cln_b[l], w_conv_b[l], w_o_b[l], ln1_g[l], ln1_b[l],
               wq_b[l], keys_b[l], u_b[l], v_b[l], ln2_g[l], ln2_b[l], w_ple_b[l], w_pg_b[l])
        zero_conv = jnp.zeros((xp.shape[0], CONV_W - 1, w_dw.shape[2]), xp.dtype)
        xp, kp, vp, cp = _layer(xp, p_prompt[l], pos_p, None, None, zero_conv, l, depth, wts)
        xs, ks, vs, cs = _layer(xs, p_sample[l], pos_s, cache_k, cache_v, state_conv[l],
                                l, depth, wts)
        for lst, val in zip(outs, (kp, vp, cp, ks, vs, cs)):
            lst.append(val)
    return (xp, xs) + tuple(jnp.stack(o) for o in outs)
```

```python
import functools
import math

import jax
import jax.numpy as jnp
from jax import lax
from jax.experimental import pallas as pl
from jax.experimental.pallas import tpu as pltpu

F32 = jnp.float32
BF16 = jnp.bfloat16

CHUNK = 64
N_HEADS = 8
HEAD_DIM = 128
HEAD_W = 2 * HEAD_DIM
ROT_DIM = HEAD_DIM // 4
ROPE_THETA = 500000.0
CONV_W = 31
CONV_HALO = 32
PEER_HEADS = 8
PEER_NKEYS = 128
PEER_TOPK = 16
LN_EPS = 1e-5
LOG2E = math.log2(math.e)
NEG_BIG = -1e30
LANES = 128
SUBLANES = 8
VMEM_LIMIT = 56 * 2**20


def _pick(n, prefs):
    for p in prefs:
        if n % p == 0:
            return p
    return n


def _params(*sem):
    return pltpu.CompilerParams(dimension_semantics=sem, vmem_limit_bytes=VMEM_LIMIT)


def _resident(shape, index_map):
    return pl.BlockSpec(shape, index_map, pipeline_mode=pl.Buffered(1))


def _layer_norm(x, g, b):
    mu = jnp.mean(x, axis=-1, keepdims=True)
    d = x - mu
    var = jnp.mean(d * d, axis=-1, keepdims=True)
    return d * lax.rsqrt(var + LN_EPS) * g + b


def _sigmoid(x):
    return 1.0 / (1.0 + jnp.exp(-x))


def _rope_proj_kernel(x_ref, w_ref, cos_ref, sa_ref, sb_ref, *out_refs):
    acc = jnp.dot(x_ref[...], w_ref[...], preferred_element_type=F32)
    width = acc.shape[1]
    reps = width // LANES
    cos = jnp.concatenate([cos_ref[...]] * reps, axis=1)
    sa = jnp.concatenate([sa_ref[...]] * reps, axis=1)
    sb = jnp.concatenate([sb_ref[...]] * reps, axis=1)
    half = ROT_DIM // 2
    y = (acc * cos + pltpu.roll(acc, width - half, 1) * sa
         + pltpu.roll(acc, half, 1) * sb)
    for o in out_refs:
        o[...] = y.astype(o.dtype)


def _plain_proj_kernel(x_ref, w_ref, *out_refs):
    acc = jnp.dot(x_ref[...], w_ref[...], preferred_element_type=F32)
    for o in out_refs:
        o[...] = acc.astype(o.dtype)


def _glu_proj_kernel(x_ref, wa_ref, wg_ref, o_ref):
    x = x_ref[...]
    a = jnp.dot(x, wa_ref[...], preferred_element_type=F32)
    g = jnp.dot(x, wg_ref[...], preferred_element_type=F32)
    o_ref[...] = a * _sigmoid(g)


def _sigmoid_proj_kernel(x_ref, w_ref, o_ref):
    acc = jnp.dot(x_ref[...], w_ref[...], preferred_element_type=F32)
    o_ref[...] = _sigmoid(acc)


def _proj_tiles(n, width):
    tm = _pick(n, (1024, 512, 256, 128, 64, 32, 16, 8))
    tn = _pick(width, (1024, 512, 256, 128))
    return tm, tn


def _rope_proj(xb, w, col0, width, tabs, out_dtypes):
    n, d = xb.shape
    tm, tn = _proj_tiles(n, width)
    tab_blocks = tabs[0].shape[0] // tm
    grid = (width // tn, n // tm)
    tab_spec = pl.BlockSpec((tm, LANES), lambda j, i: (i % tab_blocks, 0))
    return pl.pallas_call(
        _rope_proj_kernel,
        grid=grid,
        in_specs=[pl.BlockSpec((tm, d), lambda j, i: (i, 0)),
                  pl.BlockSpec((d, tn), lambda j, i: (0, col0 // tn + j)),
                  tab_spec, tab_spec, tab_spec],
        out_specs=[pl.BlockSpec((tm, tn), lambda j, i: (i, j)) for _ in out_dtypes],
        out_shape=[jax.ShapeDtypeStruct((n, width), dt) for dt in out_dtypes],
        compiler_params=_params("parallel", "parallel"),
        name="rope_proj",
    )(xb, w, *tabs)


def _plain_proj(xb, w, col0, width, out_dtypes):
    n, d = xb.shape
    tm, tn = _proj_tiles(n, width)
    return pl.pallas_call(
        _plain_proj_kernel,
        grid=(width // tn, n // tm),
        in_specs=[pl.BlockSpec((tm, d), lambda j, i: (i, 0)),
                  pl.BlockSpec((d, tn), lambda j, i: (0, col0 // tn + j))],
        out_specs=[pl.BlockSpec((tm, tn), lambda j, i: (i, j)) for _ in out_dtypes],
        out_shape=[jax.ShapeDtypeStruct((n, width), dt) for dt in out_dtypes],
        compiler_params=_params("parallel", "parallel"),
        name="plain_proj",
    )(xb, w)


def _glu_proj(xb, w, col0, width):
    n, d = xb.shape
    tm, tn = _proj_tiles(n, width)
    return pl.pallas_call(
        _glu_proj_kernel,
        grid=(width // tn, n // tm),
        in_specs=[pl.BlockSpec((tm, d), lambda j, i: (i, 0)),
                  pl.BlockSpec((d, tn), lambda j, i: (0, col0 // tn + j)),
                  pl.BlockSpec((d, tn), lambda j, i: (0, (col0 + width) // tn + j))],
        out_specs=pl.BlockSpec((tm, tn), lambda j, i: (i, j)),
        out_shape=jax.ShapeDtypeStruct((n, width), F32),
        compiler_params=_params("parallel", "parallel"),
        name="glu_proj",
    )(xb, w, w)


def _sigmoid_proj(xb, w, col0, width):
    n, d = xb.shape
    tm, tn = _proj_tiles(n, width)
    return pl.pallas_call(
        _sigmoid_proj_kernel,
        grid=(width // tn, n // tm),
        in_specs=[pl.BlockSpec((tm, d), lambda j, i: (i, 0)),
                  pl.BlockSpec((d, tn), lambda j, i: (0, col0 // tn + j))],
        out_specs=pl.BlockSpec((tm, tn), lambda j, i: (i, j)),
        out_shape=jax.ShapeDtypeStruct((n, width), F32),
        compiler_params=_params("parallel", "parallel"),
        name="gate_proj",
    )(xb, w)


def _lambda_value(lq_ref, lk_ref, lam_init):
    lf = lq_ref[...] * lk_ref[...]
    e = jnp.exp(jnp.sum(lf, axis=1, keepdims=True))
    return e[0:1, :] - e[1:2, :] + lam_init


def _finish_heads(o1, o2, lam, g_ref, lam_init):
    o = o1 - lam * o2
    ms = jnp.mean(o * o, axis=-1, keepdims=True)
    return o * lax.rsqrt(ms + LN_EPS) * g_ref[...] * (1.0 - lam_init)


ATTN_HEADS_PER_STEP = 2


def _attn_prompt_kernel(lq_ref, lk_ref, g_ref, q_ref, k_ref, v_ref, o_ref,
                        m_ref, l_ref, acc_ref, *, tq, lam_init):
    i = pl.program_id(2)
    c = (HEAD_DIM ** -0.5) * LOG2E
    chains = 2 * ATTN_HEADS_PER_STEP
    m_ref[...] = jnp.full(m_ref.shape, NEG_BIG, F32)
    l_ref[...] = jnp.zeros(l_ref.shape, F32)
    acc_ref[...] = jnp.zeros(acc_ref.shape, F32)

    def step(kb, mask):
        r0 = pl.multiple_of(kb * tq, tq)
        for ch in range(chains):
            hh, comp = divmod(ch, 2)
            qk_cols = slice(hh * HEAD_W + comp * HEAD_DIM, hh * HEAD_W + (comp + 1) * HEAD_DIM)
            s = lax.dot_general(q_ref[:, qk_cols], k_ref[pl.ds(r0, tq), qk_cols],
                                (((1,), (1,)), ((), ())), preferred_element_type=F32)
            if mask is not None:
                s = jnp.where(mask, s, NEG_BIG)
            m_prev = m_ref[ch]
            m_new = jnp.maximum(m_prev, jnp.max(s, axis=1, keepdims=True))
            alpha = jnp.exp2((m_prev - m_new) * c)
            p = jnp.exp2((s - jnp.concatenate([m_new] * (tq // LANES), axis=1)) * c)
            l_ref[ch] = alpha * l_ref[ch] + jnp.sum(p, axis=1, keepdims=True)
            v = v_ref[pl.ds(r0, tq), hh * HEAD_W:(hh + 1) * HEAD_W]
            acc_ref[ch] = (acc_ref[ch] * jnp.concatenate([alpha] * (HEAD_W // LANES), axis=1)
                           + jnp.dot(p.astype(BF16), v, preferred_element_type=F32))
            m_ref[ch] = m_new

    def body(kb, carry):
        step(kb, None)
        return carry

    lax.fori_loop(0, i, body, 0)
    qc = lax.broadcasted_iota(jnp.int32, (tq, tq), 0) // CHUNK
    kc = lax.broadcasted_iota(jnp.int32, (tq, tq), 1) // CHUNK
    step(i, kc <= qc)

    lam = _lambda_value(lq_ref, lk_ref, lam_init)
    rep = HEAD_W // LANES
    for hh in range(ATTN_HEADS_PER_STEP):
        o1 = acc_ref[2 * hh] / jnp.concatenate([l_ref[2 * hh]] * rep, axis=1)
        o2 = acc_ref[2 * hh + 1] / jnp.concatenate([l_ref[2 * hh + 1]] * rep, axis=1)
        o_ref[:, hh * HEAD_W:(hh + 1) * HEAD_W] = _finish_heads(
            o1, o2, lam, g_ref, lam_init).astype(o_ref.dtype)


def _attn_prompt(qb, kb, vb, lam_q, lam_k, subln_g, lam_init):
    b, s, w = qb.shape
    tq = _pick(s, (512, 256, 128, 64))
    hw = ATTN_HEADS_PER_STEP * HEAD_W
    chains = 2 * ATTN_HEADS_PER_STEP
    kern = functools.partial(_attn_prompt_kernel, tq=tq, lam_init=lam_init)
    small = lambda shape: pl.BlockSpec(shape, lambda bi, h, i: (0, 0))
    seq = lambda: pl.BlockSpec((None, s, hw), lambda bi, h, i: (bi, 0, h), pipeline_mode=pl.Buffered(1))
    return pl.pallas_call(
        kern,
        grid=(b, N_HEADS // ATTN_HEADS_PER_STEP, s // tq),
        in_specs=[small((2, HEAD_DIM)), small((2, HEAD_DIM)), small((1, HEAD_W)),
                  pl.BlockSpec((None, tq, hw), lambda bi, h, i: (bi, i, h)),
                  seq(), seq()],
        out_specs=pl.BlockSpec((None, tq, hw), lambda bi, h, i: (bi, i, h)),
        out_shape=jax.ShapeDtypeStruct((b, s, w), BF16),
        scratch_shapes=[pltpu.VMEM((chains, tq, LANES), F32), pltpu.VMEM((chains, tq, LANES), F32),
                        pltpu.VMEM((chains, tq, HEAD_W), F32)],
        compiler_params=_params("parallel", "parallel", "arbitrary"),
        name="attn_prompt",
    )(lam_q, lam_k, subln_g.reshape(1, HEAD_W), qb, kb, vb)


def _attn_sample_kernel(lq_ref, lk_ref, g_ref, q_ref, kn_ref, vn_ref, kp_ref, vp_ref, o_ref,
                        qbd_ref, bias_ref, m_ref, l_ref, acc_ref, *, lam_init, t):
    j = pl.program_id(1)
    c = (HEAD_DIM ** -0.5) * LOG2E
    rows = N_HEADS * 2 * t

    def attend(k_flat, v_flat):
        n = k_flat.shape[0]
        s = lax.dot_general(qbd_ref[...], k_flat, (((1,), (1,)), ((), ())),
                            preferred_element_type=F32)
        s = s + jnp.concatenate([bias_ref[...]] * (n // LANES), axis=1)
        m_prev = m_ref[...]
        m_new = jnp.maximum(m_prev, jnp.max(s, axis=1, keepdims=True))
        alpha = jnp.exp2((m_prev - m_new) * c)
        p = jnp.exp2((s - jnp.concatenate([m_new] * (n // LANES), axis=1)) * c)
        l_ref[...] = alpha * l_ref[...] + jnp.sum(p, axis=1, keepdims=True)
        acc_ref[...] = (acc_ref[...] * jnp.concatenate([alpha] * (HEAD_W // LANES), axis=1)
                        + jnp.dot(p.astype(BF16), v_flat, preferred_element_type=F32))
        m_ref[...] = m_new

    @pl.when(j == 0)
    def _():
        qbd_ref[...] = jnp.zeros(qbd_ref.shape, BF16)
        for h in range(N_HEADS):
            for comp in range(2):
                r0 = (h * 2 + comp) * t
                cols = slice(comp * HEAD_DIM, (comp + 1) * HEAD_DIM)
                qbd_ref[r0:r0 + t, cols] = q_ref[:, h * HEAD_W + comp * HEAD_DIM:
                                                 h * HEAD_W + (comp + 1) * HEAD_DIM]
        row_head = lax.broadcasted_iota(jnp.int32, (rows, LANES), 0) // (2 * t)
        key_head = lax.broadcasted_iota(jnp.int32, (rows, LANES), 1) % N_HEADS
        bias_ref[...] = jnp.where(row_head == key_head, 0.0, NEG_BIG)
        m_ref[...] = jnp.full(m_ref.shape, NEG_BIG, F32)
        l_ref[...] = jnp.zeros(l_ref.shape, F32)
        acc_ref[...] = jnp.zeros(acc_ref.shape, F32)
        attend(kn_ref[...].reshape(t * N_HEADS, HEAD_W).astype(BF16),
               vn_ref[...].reshape(t * N_HEADS, HEAD_W).astype(BF16))

    pb = kp_ref.shape[0]
    attend(kp_ref[...].reshape(pb * N_HEADS, HEAD_W).astype(BF16),
           vp_ref[...].reshape(pb * N_HEADS, HEAD_W).astype(BF16))

    @pl.when(j == pl.num_programs(1) - 1)
    def _():
        lam = _lambda_value(lq_ref, lk_ref, lam_init)
        rep = HEAD_W // LANES
        o = acc_ref[...] / jnp.concatenate([l_ref[...]] * rep, axis=1)
        for h in range(N_HEADS):
            r0 = h * 2 * t
            o_ref[:, h * HEAD_W:(h + 1) * HEAD_W] = _finish_heads(
                o[r0:r0 + t], o[r0 + t:r0 + 2 * t], lam, g_ref, lam_init).astype(o_ref.dtype)


def _attn_sample(qb, k_new, v_new, k_cache, v_cache, layer_idx, lam_q, lam_k, subln_g, lam_init):
    b, t, w = qb.shape
    p = k_cache.shape[2]
    pb = _pick(p, (512, 256, 128, 64, 32, 16))
    rows = N_HEADS * 2 * t
    kern = functools.partial(_attn_sample_kernel, lam_init=lam_init, t=t)
    small = lambda shape: pl.BlockSpec(shape, lambda bi, j: (0, 0))
    new = pl.BlockSpec((None, t, N_HEADS, HEAD_W), lambda bi, j: (bi, 0, 0, 0))
    past = pl.BlockSpec((None, None, pb, N_HEADS, HEAD_W), lambda bi, j: (layer_idx, bi, j, 0, 0))
    return pl.pallas_call(
        kern,
        grid=(b, p // pb),
        in_specs=[small((2, HEAD_DIM)), small((2, HEAD_DIM)), small((1, HEAD_W)),
                  pl.BlockSpec((None, t, w), lambda bi, j: (bi, 0, 0)),
                  new, new, past, past],
        out_specs=pl.BlockSpec((None, t, w), lambda bi, j: (bi, 0, 0)),
        out_shape=jax.ShapeDtypeStruct((b, t, w), BF16),
        scratch_shapes=[pltpu.VMEM((rows, HEAD_W), BF16), pltpu.VMEM((rows, LANES), F32),
                        pltpu.VMEM((rows, LANES), F32), pltpu.VMEM((rows, LANES), F32),
                        pltpu.VMEM((rows, HEAD_W), F32)],
        compiler_params=_params("parallel", "arbitrary"),
        name="attn_sample",
    )(lam_q, lam_k, subln_g.reshape(1, HEAD_W), qb, k_new, v_new, k_cache, v_cache)


def _conv_kernel(*refs, tm, has_halo):
    if has_halo:
        (past_ref, halo_ref, cur_ref, w_ref, b_ref, g_ref, beta_ref, o_ref,
         win_ref, sh_ref, y_ref) = refs
    else:
        past_ref, cur_ref, w_ref, b_ref, g_ref, beta_ref, o_ref, win_ref, sh_ref, y_ref = refs
        halo_ref = None
    i = pl.program_id(1)
    if has_halo:
        @pl.when(i == 0)
        def _():
            win_ref[0:CONV_HALO, :] = past_ref[...]

        @pl.when(i > 0)
        def _():
            win_ref[0:CONV_HALO, :] = halo_ref[...]
    else:
        win_ref[0:CONV_HALO, :] = past_ref[...]
    win_ref[CONV_HALO:, :] = cur_ref[...]

    width = cur_ref.shape[1]
    rc = _pick(tm, (32, 16, 8))
    lead = CONV_HALO - (CONV_W - 1)
    sh_rows = sh_ref.shape[1]

    def col_body(cj, carry):
        c0 = pl.multiple_of(cj * LANES, LANES)
        wk = [jnp.broadcast_to(w_ref[pl.ds(k, 1), pl.ds(c0, LANES)], (rc, LANES))
              for k in range(CONV_W)]
        bias = jnp.broadcast_to(b_ref[:, pl.ds(c0, LANES)], (rc, LANES))
        for s in range(1, SUBLANES):
            sh_ref[s - 1] = win_ref[pl.ds(s, sh_rows), pl.ds(c0, LANES)]
        for r in range(tm // rc):
            acc = bias
            for k in range(CONV_W):
                q, s = divmod(lead + k, SUBLANES)
                row = r * rc + q * SUBLANES
                if s == 0:
                    tap = win_ref[pl.ds(row, rc), pl.ds(c0, LANES)]
                else:
                    tap = sh_ref[s - 1, pl.ds(row, rc), :]
                acc = acc + tap * wk[k]
            y_ref[pl.ds(r * rc, rc), pl.ds(c0, LANES)] = acc
        return carry

    lax.fori_loop(0, width // LANES, col_body, 0)
    y = _layer_norm(y_ref[...], g_ref[...], beta_ref[...])
    o_ref[...] = (y * _sigmoid(y)).astype(o_ref.dtype)


def _conv_branch(c, past32, w_dw32, b_dw, cln_g, cln_b):
    b, t, ch = c.shape
    tm = _pick(t, (256, 128, 64, 32, 16, 8))
    has_halo = t > tm
    kern = functools.partial(_conv_kernel, tm=tm, has_halo=has_halo)
    vec = lambda: pl.BlockSpec((1, ch), lambda bi, i: (0, 0))
    hb = tm // CONV_HALO
    in_specs = [pl.BlockSpec((None, CONV_HALO, ch), lambda bi, i: (bi, 0, 0))]
    args = [past32]
    if has_halo:
        in_specs.append(pl.BlockSpec((None, CONV_HALO, ch),
                                     lambda bi, i: (bi, jnp.maximum(i * hb - 1, 0), 0)))
        args.append(c)
    in_specs += [pl.BlockSpec((None, tm, ch), lambda bi, i: (bi, i, 0)),
                 pl.BlockSpec((CONV_HALO, ch), lambda bi, i: (0, 0)),
                 vec(), vec(), vec()]
    args += [c, w_dw32, b_dw.reshape(1, ch), cln_g.reshape(1, ch), cln_b.reshape(1, ch)]
    return pl.pallas_call(
        kern,
        grid=(b, t // tm),
        in_specs=in_specs,
        out_specs=pl.BlockSpec((None, tm, ch), lambda bi, i: (bi, i, 0)),
        out_shape=jax.ShapeDtypeStruct((b, t, ch), BF16),
        scratch_shapes=[pltpu.VMEM((tm + CONV_HALO, ch), F32),
                        pltpu.VMEM((SUBLANES - 1, tm + CONV_HALO - SUBLANES, LANES), F32),
                        pltpu.VMEM((tm, ch), F32)],
        compiler_params=_params("parallel", "parallel"),
        name="conv_branch",
    )(*args)


def _mix_kernel(on_ref, ca_ref, ga_ref, gc_ref, wa_ref, wc_ref, o_ref):
    att = jnp.dot(on_ref[...], wa_ref[...], preferred_element_type=F32)
    cv = jnp.dot(ca_ref[...], wc_ref[...], preferred_element_type=F32)
    o_ref[...] = (ga_ref[...] * att + gc_ref[...] * cv).astype(o_ref.dtype)


def _branch_mix(o_n, cact, gates, w_att, w_conv):
    n, d = o_n.shape
    dm = w_att.shape[1]
    tm = _pick(n, (512, 256, 128, 64, 32, 16, 8))
    tn = _pick(dm, (1024, 512, 256, 128))
    nj = dm // tn
    return pl.pallas_call(
        _mix_kernel,
        grid=(nj, n // tm),
        in_specs=[pl.BlockSpec((tm, d), lambda j, i: (i, 0)),
                  pl.BlockSpec((tm, cact.shape[1]), lambda j, i: (i, 0)),
                  pl.BlockSpec((tm, tn), lambda j, i: (i, j)),
                  pl.BlockSpec((tm, tn), lambda j, i: (i, nj + j)),
                  pl.BlockSpec((d, tn), lambda j, i: (0, j)),
                  pl.BlockSpec((cact.shape[1], tn), lambda j, i: (0, j))],
        out_specs=pl.BlockSpec((tm, tn), lambda j, i: (i, j)),
        out_shape=jax.ShapeDtypeStruct((n, dm), BF16),
        compiler_params=_params("parallel", "parallel"),
        name="branch_mix",
    )(o_n, cact, gates, gates, w_att, w_conv)


def _out_ln_kernel(mix_ref, x_ref, w_ref, g_ref, b_ref, o_ref, ob_ref, *, alpha):
    y = alpha * x_ref[...] + jnp.dot(mix_ref[...], w_ref[...], preferred_element_type=F32)
    y = _layer_norm(y, g_ref[...], b_ref[...])
    o_ref[...] = y
    ob_ref[...] = y.astype(BF16)


def _out_ln(mix, x, w_o, g, b, alpha):
    n, d = x.shape
    tm = _pick(n, (512, 256, 128, 64, 32, 16, 8))
    vec = lambda: pl.BlockSpec((1, d), lambda i: (0, 0))
    return pl.pallas_call(
        functools.partial(_out_ln_kernel, alpha=alpha),
        grid=(n // tm,),
        in_specs=[pl.BlockSpec((tm, d), lambda i: (i, 0)),
                  pl.BlockSpec((tm, d), lambda i: (i, 0)),
                  _resident((d, d), lambda i: (0, 0)),
                  vec(), vec()],
        out_specs=[pl.BlockSpec((tm, d), lambda i: (i, 0)),
                   pl.BlockSpec((tm, d), lambda i: (i, 0))],
        out_shape=[jax.ShapeDtypeStruct((n, d), F32), jax.ShapeDtypeStruct((n, d), BF16)],
        compiler_params=_params("parallel"),
        name="out_ln1",
    )(mix, x, w_o, g.reshape(1, d), b.reshape(1, d))


def _peer_candidate_plan():
    return [PEER_TOPK // (a + 1) for a in range(PEER_TOPK)]


def _top_rows(s, count):
    rows, t = s.shape
    ridx = lax.broadcasted_iota(jnp.int32, (count, t), 0)
    out = jnp.zeros((count, t), F32)
    cur = s
    for r in range(count):
        m = jnp.max(cur, axis=0, keepdims=True)
        out = jnp.where(ridx == r, m, out)
        if r + 1 < count:
            cur = jnp.where(cur == m, -jnp.inf, cur)
    return out


def _sort_network(n):
    pairs = []
    p = 1
    while p < n:
        k = p
        while k >= 1:
            for j in range(k % p, n - k, 2 * k):
                for i in range(min(k, n - j - k)):
                    if (i + j) // (2 * p) == (i + j + k) // (2 * p):
                        pairs.append((i + j, i + j + k))
            k //= 2
        p *= 2
    return pairs


def _bitonic_network(n):
    pairs = []
    d = n // 2
    while d >= 1:
        for i in range(n):
            if (i // d) % 2 == 0:
                pairs.append((i, i + d))
        d //= 2
    return pairs


def _exchange(vals, pairs):
    for a, b in pairs:
        hi, lo = jnp.maximum(vals[a], vals[b]), jnp.minimum(vals[a], vals[b])
        vals[a], vals[b] = hi, lo
    return vals


def _top16_of_128(s):
    n = PEER_TOPK
    vals = _exchange([s[SUBLANES * k:SUBLANES * (k + 1), :] for k in range(n)], _sort_network(n))
    for shift in (4, 2, 1):
        other = [pltpu.roll(v, shift, 0) for v in vals]
        vals = [jnp.maximum(vals[k], other[n - 1 - k]) for k in range(n)]
        vals = _exchange(vals, _bitonic_network(n))
    row = lax.broadcasted_iota(jnp.int32, vals[0].shape, 0)
    groups = []
    for g in range(n // SUBLANES):
        acc = vals[g * SUBLANES]
        for k in range(1, SUBLANES):
            acc = jnp.where(row == k, vals[g * SUBLANES + k], acc)
        groups.append(acc)
    return jnp.concatenate(groups, axis=0), vals


def _peer_route_kernel(x_ref, wq_ref, keys_ref, lt_ref, rb_ref, e1_ref, e2_ref):
    q = jnp.dot(x_ref[...], wq_ref[...], preferred_element_type=F32).astype(BF16)
    tn = q.shape[0]
    limits = _peer_candidate_plan()
    row8 = lax.broadcasted_iota(jnp.int32, (8, tn), 0)
    for h in range(PEER_HEADS):
        st = []
        for half in range(2):
            col = (h * 2 + half) * PEER_NKEYS
            s = lax.dot_general(keys_ref[h, half], q[:, col:col + PEER_NKEYS],
                                (((1,), (1,)), ((), ())), preferred_element_type=F32)
            st.append(s * LOG2E)
        v1, v1_rows = _top16_of_128(st[0])
        v2, v2_rows = _top16_of_128(st[1])
        groups = [v1[0:1, :] + v2[0:8, :], v1[0:1, :] + v2[8:16, :]]
        for a in range(1, 8):
            cand = v1[a:a + 1, :] + v2[0:8, :]
            if limits[a] < 8:
                cand = jnp.where(row8 < limits[a], cand, -jnp.inf)
            groups.append(cand)
        groups.append(v1[8:16, :] + v2[0:1, :])
        cand = jnp.concatenate(groups, axis=0)
        top = _top_rows(cand, PEER_TOPK)
        z = jnp.sum(jnp.exp2(top - top[0:1, :]), axis=0, keepdims=True)
        tau = top[PEER_TOPK - 1:PEER_TOPK, :]
        count_by_rank = jnp.zeros((PEER_TOPK, tn), F32)
        for b in range(PEER_TOPK):
            count_by_rank = count_by_rank + jnp.where(v1 + v2[b:b + 1, :] >= tau, 1.0, 0.0)
        count_rows = [jnp.broadcast_to(count_by_rank[a:a + 1, :], (SUBLANES, tn))
                      for a in range(PEER_TOPK)]
        counts, ranks = [], []
        for k in range(PEER_NKEYS // SUBLANES):
            rows = slice(k * SUBLANES, (k + 1) * SUBLANES)
            s1k, s2k = st[0][rows, :], st[1][rows, :]
            count = jnp.zeros((SUBLANES, tn), F32)
            rank = jnp.full((SUBLANES, tn), float(PEER_TOPK), F32)
            for a in range(PEER_TOPK):
                count = jnp.where(s1k == v1_rows[a], count_rows[a], count)
                rank = jnp.where(s2k == v2_rows[a], float(a), rank)
            counts.append(count)
            ranks.append(rank)
        lt_ref[h] = jnp.concatenate(counts, axis=0)
        rb_ref[h] = jnp.concatenate(ranks, axis=0).astype(BF16)
        e1_ref[h] = jnp.exp2(st[0] - v1[0:1, :])
        e2_ref[h] = (jnp.exp2(st[1] - v2[0:1, :]) / z).astype(BF16)


def _peer_route(xb, wq, keys):
    n, d = xb.shape
    tn = _pick(n, (256, 128))
    qw = wq.shape[1]
    tab = lambda: pl.BlockSpec((PEER_HEADS, PEER_NKEYS, tn), lambda i: (0, 0, i))
    shape = lambda dt: jax.ShapeDtypeStruct((PEER_HEADS, PEER_NKEYS, n), dt)
    return pl.pallas_call(
        _peer_route_kernel,
        grid=(n // tn,),
        in_specs=[pl.BlockSpec((tn, d), lambda i: (i, 0)),
                  _resident((d, qw), lambda i: (0, 0)),
                  _resident(keys.shape, lambda i: (0, 0, 0, 0))],
        out_specs=[tab(), tab(), tab(), tab()],
        out_shape=[shape(F32), shape(BF16), shape(F32), shape(BF16)],
        compiler_params=_params("parallel"),
        name="peer_route",
    )(xb, wq, keys)


def _gelu_tanh(x):
    c = -2.0 * math.sqrt(2.0 / math.pi) * LOG2E
    w = x * ((x * x) * (0.044715 * c) + c)
    return x / (1.0 + jnp.exp2(w))


def _rows_bf16(row):
    packed = jnp.broadcast_to(row, (2 * SUBLANES, LANES)).astype(BF16)
    return jnp.concatenate([packed] * (PEER_NKEYS // (2 * SUBLANES)), axis=0)


PEER_HALF = 512
PEER_STEP = 2 * PEER_HALF


def _peer_mix_kernel(x_ref, lt_ref, rb_ref, e1_ref, e2_ref, u_ref, vt_ref, g_ref, b_ref,
                     o_ref, xb_ref, acc_ref, ht_ref, act_ref, *, alpha):
    eb = pl.program_id(1)
    tn = x_ref.shape[0]
    groups = PEER_HALF // PEER_NKEYS

    @pl.when(eb == 0)
    def _():
        xb_ref[...] = x_ref[...].T.astype(BF16)
        acc_ref[...] = jnp.zeros(acc_ref.shape, F32)

    for hf in range(2):
        ht_ref[hf] = jnp.dot(u_ref[hf * PEER_HALF:(hf + 1) * PEER_HALF, :], xb_ref[...],
                             preferred_element_type=F32)

    row8 = pl.multiple_of(eb * 8, 8)
    for hf in range(2):
        for tc in range(tn // LANES):
            cs = slice(tc * LANES, (tc + 1) * LANES)
            c_grp = [lt_ref[h, pl.ds(row8, 8), cs] for h in range(PEER_HEADS)]
            e_grp = [e1_ref[h, pl.ds(row8, 8), cs] for h in range(PEER_HEADS)]
            for ii in range(groups):
                r = hf * groups + ii
                gsum = jnp.zeros((PEER_NKEYS, LANES), BF16)
                for h in range(PEER_HEADS):
                    chosen = rb_ref[h, :, cs] < _rows_bf16(c_grp[h][r:r + 1])
                    gate = jnp.where(chosen, e2_ref[h, :, cs], 0.0) * _rows_bf16(e_grp[h][r:r + 1])
                    gsum = gsum + gate
                rs = slice(ii * PEER_NKEYS, (ii + 1) * PEER_NKEYS)
                act_ref[hf, rs, cs] = _gelu_tanh(ht_ref[hf, rs, cs]).astype(BF16) * gsum
        acc_ref[...] += jnp.dot(vt_ref[:, hf * PEER_HALF:(hf + 1) * PEER_HALF], act_ref[hf],
                                preferred_element_type=F32)

    @pl.when(eb == pl.num_programs(1) - 1)
    def _():
        y = alpha * x_ref[...] + acc_ref[...].T
        o_ref[...] = _layer_norm(y, g_ref[...], b_ref[...])


def _peer_mix(x, tabs, u, vt, g, b, alpha):
    n, d = x.shape
    e = u.shape[0]
    assert e == PEER_NKEYS * PEER_NKEYS and e % PEER_STEP == 0
    count1, rank2, e1, e2 = tabs
    tn = _pick(n, (512, 256, 128))
    once = lambda shape, imap: pl.BlockSpec(shape, imap, pipeline_mode=pl.Buffered(1))
    tab = lambda: once((PEER_HEADS, PEER_NKEYS, tn), lambda t, j: (0, 0, t))
    vec = lambda: pl.BlockSpec((1, d), lambda t, j: (0, 0))
    kern = functools.partial(_peer_mix_kernel, alpha=alpha)
    return pl.pallas_call(
        kern,
        grid=(n // tn, e // PEER_STEP),
        in_specs=[once((tn, d), lambda t, j: (t, 0)),
                  tab(), tab(), tab(), tab(),
                  pl.BlockSpec((PEER_STEP, d), lambda t, j: (j, 0)),
                  pl.BlockSpec((d, PEER_STEP), lambda t, j: (0, j)),
                  vec(), vec()],
        out_specs=pl.BlockSpec((tn, d), lambda t, j: (t, 0)),
        out_shape=jax.ShapeDtypeStruct((n, d), F32),
        scratch_shapes=[pltpu.VMEM((d, tn), BF16), pltpu.VMEM((d, tn), F32),
                        pltpu.VMEM((2, PEER_HALF, tn), F32), pltpu.VMEM((2, PEER_HALF, tn), BF16)],
        compiler_params=_params("parallel", "arbitrary"),
        name="peer_mix",
    )(x, count1, rank2, e1, e2, u, vt, g.reshape(1, d), b.reshape(1, d))


def _ple_kernel(xr_ref, xc_ref, pe_ref, wg_ref, wp_ref, o_ref):
    gate = _sigmoid(jnp.dot(xr_ref[...].astype(BF16), wg_ref[...], preferred_element_type=F32))
    emb = jnp.dot(pe_ref[...].astype(BF16), wp_ref[...], preferred_element_type=F32)
    o_ref[...] = xc_ref[...] + gate * emb


def _ple(x, pe, w_gate, w_ple):
    n, d = x.shape
    pd = pe.shape[1]
    tm = _pick(n, (512, 256, 128, 64, 32, 16, 8))
    tn = _pick(d, (1024, 512, 256, 128))
    return pl.pallas_call(
        _ple_kernel,
        grid=(d // tn, n // tm),
        in_specs=[pl.BlockSpec((tm, d), lambda j, i: (i, 0)),
                  pl.BlockSpec((tm, tn), lambda j, i: (i, j)),
                  pl.BlockSpec((tm, pd), lambda j, i: (i, 0)),
                  pl.BlockSpec((d, tn), lambda j, i: (0, j)),
                  pl.BlockSpec((pd, tn), lambda j, i: (0, j))],
        out_specs=pl.BlockSpec((tm, tn), lambda j, i: (i, j)),
        out_shape=jax.ShapeDtypeStruct((n, d), F32),
        compiler_params=_params("parallel", "parallel"),
        name="ple",
    )(x, x, pe, w_gate, w_ple)


def _rope_tables(pos, rows_per_block_hint):
    inv = ROPE_THETA ** (-jnp.arange(0, ROT_DIM, 2, dtype=F32) / ROT_DIM)
    ang = pos.astype(F32)[:, None] * inv[None, :]
    cos, sin = jnp.cos(ang), jnp.sin(ang)
    t = pos.shape[0]
    half = ROT_DIM // 2
    ones = jnp.ones((t, HEAD_DIM - ROT_DIM), F32)
    zeros_h = jnp.zeros((t, half), F32)
    zeros_r = jnp.zeros((t, HEAD_DIM - ROT_DIM), F32)
    c = jnp.concatenate([cos, cos, ones], axis=1)
    sa = jnp.concatenate([-sin, zeros_h, zeros_r], axis=1)
    sb = jnp.concatenate([zeros_h, sin, zeros_r], axis=1)
    if t < rows_per_block_hint:
        reps = rows_per_block_hint // t
        c, sa, sb = (jnp.tile(a, (reps, 1)) for a in (c, sa, sb))
    return c, sa, sb


def _layer(x, pe, pos, k_past, v_past, conv_past, layer_idx, depth, wts):
    (w_in, lam_q, lam_k, subln_g, w_att_out, w_dw, b_dw, cln_g, cln_b, w_conv_out, w_o,
     ln1_g, ln1_b, peer_wq, peer_keys, peer_u, peer_v, ln2_g, ln2_b, w_ple, w_ple_gate) = wts
    b, t, d = x.shape
    n = b * t
    att_w = N_HEADS * HEAD_W
    conv_dim = w_dw.shape[1]
    alpha = (2 * depth) ** 0.25
    lam_init = 0.8 - 0.6 * math.exp(-0.3 * layer_idx)

    x2d = x.reshape(n, d)
    xb = x2d.astype(BF16)
    tm, _ = _proj_tiles(n, att_w)
    tabs = _rope_tables(pos, tm)

    (qb,) = _rope_proj(xb, w_in, 0, att_w, tabs, (BF16,))
    c = _glu_proj(xb, w_in, 3 * att_w, conv_dim)
    gates = _sigmoid_proj(xb, w_in, 3 * att_w + 2 * conv_dim, 2 * d)

    shp = (b, t, att_w)
    heads = (b, t, N_HEADS, HEAD_W)
    if k_past is None:
        k_f32, kb = _rope_proj(xb, w_in, att_w, att_w, tabs, (F32, BF16))
        v_f32, vb = _plain_proj(xb, w_in, 2 * att_w, att_w, (F32, BF16))
        o_n = _attn_prompt(qb.reshape(shp), kb.reshape(shp), vb.reshape(shp),
                           lam_q, lam_k, subln_g, lam_init)
    else:
        (k_f32,) = _rope_proj(xb, w_in, att_w, att_w, tabs, (F32,))
        (v_f32,) = _plain_proj(xb, w_in, 2 * att_w, att_w, (F32,))
        o_n = _attn_sample(qb.reshape(shp), k_f32.reshape(heads), v_f32.reshape(heads),
                           k_past, v_past, layer_idx, lam_q, lam_k, subln_g, lam_init)

    c3 = c.reshape(b, t, conv_dim)
    past32 = jnp.pad(conv_past, ((0, 0), (CONV_HALO - (CONV_W - 1), 0), (0, 0)))
    w_dw32 = jnp.pad(w_dw, ((0, CONV_HALO - CONV_W), (0, 0)))
    cact = _conv_branch(c3, past32, w_dw32, b_dw, cln_g, cln_b)
    conv_state = jnp.concatenate([conv_past, c3], axis=1)[:, -(CONV_W - 1):]

    mix = _branch_mix(o_n.reshape(n, att_w), cact.reshape(n, conv_dim), gates, w_att_out, w_conv_out)
    x1, x1b = _out_ln(mix, x2d, w_o, ln1_g, ln1_b, alpha)

    route = _peer_route(x1b, peer_wq, peer_keys)
    x2 = _peer_mix(x1, route, peer_u, peer_v, ln2_g, ln2_b, alpha)

    y = _ple(x2, pe.reshape(n, pe.shape[-1]), w_ple_gate, w_ple)
    return (y.reshape(b, t, d), k_f32.reshape(b, t, N_HEADS, HEAD_W),
            v_f32.reshape(b, t, N_HEADS, HEAD_W), conv_state)


def kernel(x_prompt, x_sample, cache_k, cache_v, state_conv, p_prompt, p_sample, w_in, lam_q, lam_k, subln_g, w_att_out, w_dw, b_dw, cln_g, cln_b, w_conv_out, w_o, ln1_g, ln1_b, peer_wq, peer_keys, peer_u, peer_v, ln2_g, ln2_b, w_ple, w_ple_gate):
    depth = w_in.shape[0]
    xp, xs = x_prompt, x_sample
    pos_p = jnp.arange(x_prompt.shape[1])
    pos_s = cache_k.shape[2] + jnp.arange(x_sample.shape[1])
    mats = (w_in, w_att_out, w_conv_out, w_o, peer_wq, peer_keys, peer_u, peer_v, w_ple, w_ple_gate)
    w_in_b, w_att_b, w_conv_b, w_o_b, wq_b, keys_b, u_b, v_b, w_ple_b, w_pg_b = (
        m.astype(BF16) for m in mats)
    v_b = jnp.swapaxes(v_b, 1, 2)
    outs = [[] for _ in range(6)]
    for l in range(depth):
        wts = (w_in_b[l], lam_q[l], lam_k[l], subln_g[l], w_att_b[l], w_dw[l], b_dw[l],
               cln_g[l], cln_b[l], w_conv_b[l], w_o_b[l], ln1_g[l], ln1_b[l],
               wq_b[l], keys_b[l], u_b[l], v_b[l], ln2_g[l], ln2_b[l], w_ple_b[l], w_pg_b[l])
        zero_conv = jnp.zeros((xp.shape[0], CONV_W - 1, w_dw.shape[2]), xp.dtype)
        xp, kp, vp, cp = _layer(xp, p_prompt[l], pos_p, None, None, zero_conv, l, depth, wts)
        xs, ks, vs, cs = _layer(xs, p_sample[l], pos_s, cache_k, cache_v, state_conv[l],
                                l, depth, wts)
        for lst, val in zip(outs, (kp, vp, cp, ks, vs, cs)):
            lst.append(val)
    return (xp, xs) + tuple(jnp.stack(o) for o in outs)
```

```python
import functools
import math

import jax
import jax.numpy as jnp
from jax import lax
from jax.experimental import pallas as pl
from jax.experimental.pallas import tpu as pltpu

F32 = jnp.float32
BF16 = jnp.bfloat16

CHUNK = 64
N_HEADS = 8
HEAD_DIM = 128
HEAD_W = 2 * HEAD_DIM
ROT_DIM = HEAD_DIM // 4
ROPE_THETA = 500000.0
CONV_W = 31
CONV_HALO = 32
PEER_HEADS = 8
PEER_NKEYS = 128
PEER_TOPK = 16
LN_EPS = 1e-5
LOG2E = math.log2(math.e)
NEG_BIG = -1e30
LANES = 128
SUBLANES = 8
VMEM_LIMIT = 56 * 2**20


def _pick(n, prefs):
    for p in prefs:
        if n % p == 0:
            return p
    return n


def _params(*sem):
    return pltpu.CompilerParams(dimension_semantics=sem, vmem_limit_bytes=VMEM_LIMIT)


def _resident(shape, index_map):
    return pl.BlockSpec(shape, index_map, pipeline_mode=pl.Buffered(1))


def _layer_norm(x, g, b):
    mu = jnp.mean(x, axis=-1, keepdims=True)
    d = x - mu
    var = jnp.mean(d * d, axis=-1, keepdims=True)
    return d * lax.rsqrt(var + LN_EPS) * g + b


def _sigmoid(x):
    return 1.0 / (1.0 + jnp.exp(-x))


def _rope_proj_kernel(x_ref, w_ref, cos_ref, sa_ref, sb_ref, *out_refs):
    acc = jnp.dot(x_ref[...], w_ref[...], preferred_element_type=F32)
    width = acc.shape[1]
    reps = width // LANES
    cos = jnp.concatenate([cos_ref[...]] * reps, axis=1)
    sa = jnp.concatenate([sa_ref[...]] * reps, axis=1)
    sb = jnp.concatenate([sb_ref[...]] * reps, axis=1)
    half = ROT_DIM // 2
    y = (acc * cos + pltpu.roll(acc, width - half, 1) * sa
         + pltpu.roll(acc, half, 1) * sb)
    for o in out_refs:
        o[...] = y.astype(o.dtype)


def _plain_proj_kernel(x_ref, w_ref, *out_refs):
    acc = jnp.dot(x_ref[...], w_ref[...], preferred_element_type=F32)
    for o in out_refs:
        o[...] = acc.astype(o.dtype)


def _glu_proj_kernel(x_ref, wa_ref, wg_ref, o_ref):
    x = x_ref[...]
    a = jnp.dot(x, wa_ref[...], preferred_element_type=F32)
    g = jnp.dot(x, wg_ref[...], preferred_element_type=F32)
    o_ref[...] = a * _sigmoid(g)


def _sigmoid_proj_kernel(x_ref, w_ref, o_ref):
    acc = jnp.dot(x_ref[...], w_ref[...], preferred_element_type=F32)
    o_ref[...] = _sigmoid(acc)


def _proj_tiles(n, width):
    tm = _pick(n, (1024, 512, 256, 128, 64, 32, 16, 8))
    tn = _pick(width, (1024, 512, 256, 128))
    return tm, tn


def _rope_proj(xb, w, col0, width, tabs, out_dtypes):
    n, d = xb.shape
    tm, tn = _proj_tiles(n, width)
    tab_blocks = tabs[0].shape[0] // tm
    grid = (width // tn, n // tm)
    tab_spec = pl.BlockSpec((tm, LANES), lambda j, i: (i % tab_blocks, 0))
    return pl.pallas_call(
        _rope_proj_kernel,
        grid=grid,
        in_specs=[pl.BlockSpec((tm, d), lambda j, i: (i, 0)),
                  pl.BlockSpec((d, tn), lambda j, i: (0, col0 // tn + j)),
                  tab_spec, tab_spec, tab_spec],
        out_specs=[pl.BlockSpec((tm, tn), lambda j, i: (i, j)) for _ in out_dtypes],
        out_shape=[jax.ShapeDtypeStruct((n, width), dt) for dt in out_dtypes],
        compiler_params=_params("parallel", "parallel"),
        name="rope_proj",
    )(xb, w, *tabs)


def _plain_proj(xb, w, col0, width, out_dtypes):
    n, d = xb.shape
    tm, tn = _proj_tiles(n, width)
    return pl.pallas_call(
        _plain_proj_kernel,
        grid=(width // tn, n // tm),
        in_specs=[pl.BlockSpec((tm, d), lambda j, i: (i, 0)),
                  pl.BlockSpec((d, tn), lambda j, i: (0, col0 // tn + j))],
        out_specs=[pl.BlockSpec((tm, tn), lambda j, i: (i, j)) for _ in out_dtypes],
        out_shape=[jax.ShapeDtypeStruct((n, width), dt) for dt in out_dtypes],
        compiler_params=_params("parallel", "parallel"),
        name="plain_proj",
    )(xb, w)


def _glu_proj(xb, w, col0, width):
    n, d = xb.shape
    tm, tn = _proj_tiles(n, width)
    return pl.pallas_call(
        _glu_proj_kernel,
        grid=(width // tn, n // tm),
        in_specs=[pl.BlockSpec((tm, d), lambda j, i: (i, 0)),
                  pl.BlockSpec((d, tn), lambda j, i: (0, col0 // tn + j)),
                  pl.BlockSpec((d, tn), lambda j, i: (0, (col0 + width) // tn + j))],
        out_specs=pl.BlockSpec((tm, tn), lambda j, i: (i, j)),
        out_shape=jax.ShapeDtypeStruct((n, width), F32),
        compiler_params=_params("parallel", "parallel"),
        name="glu_proj",
    )(xb, w, w)


def _sigmoid_proj(xb, w, col0, width):
    n, d = xb.shape
    tm, tn = _proj_tiles(n, width)
    return pl.pallas_call(
        _sigmoid_proj_kernel,
        grid=(width // tn, n // tm),
        in_specs=[pl.BlockSpec((tm, d), lambda j, i: (i, 0)),
                  pl.BlockSpec((d, tn), lambda j, i: (0, col0 // tn + j))],
        out_specs=pl.BlockSpec((tm, tn), lambda j, i: (i, j)),
        out_shape=jax.ShapeDtypeStruct((n, width), F32),
        compiler_params=_params("parallel", "parallel"),
        name="gate_proj",
    )(xb, w)


def _lambda_value(lq_ref, lk_ref, lam_init):
    lf = lq_ref[...] * lk_ref[...]
    e = jnp.exp(jnp.sum(lf, axis=1, keepdims=True))
    return e[0:1, :] - e[1:2, :] + lam_init


def _finish_heads(o1, o2, lam, g_ref, lam_init):
    o = o1 - lam * o2
    ms = jnp.mean(o * o, axis=-1, keepdims=True)
    return o * lax.rsqrt(ms + LN_EPS) * g_ref[...] * (1.0 - lam_init)


ATTN_HEADS_PER_STEP = 2
ATTN_PREFETCH = 1


def _attn_prompt_kernel(lq_ref, lk_ref, g_ref, q_ref, k_ref, v_ref, o_ref,
                        m_ref, l_ref, acc_ref, s0_ref, *, tq, lam_init):
    i = pl.program_id(2)
    c = (HEAD_DIM ** -0.5) * LOG2E
    chains = 2 * ATTN_HEADS_PER_STEP
    m_ref[...] = jnp.full(m_ref.shape, NEG_BIG, F32)
    l_ref[...] = jnp.zeros(l_ref.shape, F32)
    acc_ref[...] = jnp.zeros(acc_ref.shape, F32)

    def scores(ch, kb):
        hh, comp = divmod(ch, 2)
        qk_cols = slice(hh * HEAD_W + comp * HEAD_DIM, hh * HEAD_W + (comp + 1) * HEAD_DIM)
        r0 = pl.multiple_of(kb * tq, tq)
        return lax.dot_general(q_ref[:, qk_cols], k_ref[pl.ds(r0, tq), qk_cols],
                               (((1,), (1,)), ((), ())), preferred_element_type=F32)

    for ch in range(ATTN_PREFETCH):
        s0_ref[ch] = scores(ch, 0)

    def step(kb, mask):
        r0 = pl.multiple_of(kb * tq, tq)
        for ch in range(chains):
            hh = ch // 2
            s = s0_ref[ch] if ch < ATTN_PREFETCH else scores(ch, kb)
            if mask is not None:
                s = jnp.where(mask, s, NEG_BIG)
            m_prev = m_ref[ch]
            m_new = jnp.maximum(m_prev, jnp.max(s, axis=1, keepdims=True))
            alpha = jnp.exp2((m_prev - m_new) * c)
            p = jnp.exp2((s - jnp.concatenate([m_new] * (tq // LANES), axis=1)) * c)
            l_ref[ch] = alpha * l_ref[ch] + jnp.sum(p, axis=1, keepdims=True)
            v = v_ref[pl.ds(r0, tq), hh * HEAD_W:(hh + 1) * HEAD_W]
            acc_ref[ch] = (acc_ref[ch] * jnp.concatenate([alpha] * (HEAD_W // LANES), axis=1)
                           + jnp.dot(p.astype(BF16), v, preferred_element_type=F32))
            m_ref[ch] = m_new

    def body(kb, carry):
        step(kb, None)
        for ch in range(ATTN_PREFETCH):
            s0_ref[ch] = scores(ch, kb + 1)
        return carry

    lax.fori_loop(0, i, body, 0)
    qc = lax.broadcasted_iota(jnp.int32, (tq, tq), 0) // CHUNK
    kc = lax.broadcasted_iota(jnp.int32, (tq, tq), 1) // CHUNK
    step(i, kc <= qc)

    lam = _lambda_value(lq_ref, lk_ref, lam_init)
    rep = HEAD_W // LANES
    for hh in range(ATTN_HEADS_PER_STEP):
        o1 = acc_ref[2 * hh] / jnp.concatenate([l_ref[2 * hh]] * rep, axis=1)
        o2 = acc_ref[2 * hh + 1] / jnp.concatenate([l_ref[2 * hh + 1]] * rep, axis=1)
        o_ref[:, hh * HEAD_W:(hh + 1) * HEAD_W] = _finish_heads(
            o1, o2, lam, g_ref, lam_init).astype(o_ref.dtype)


def _attn_prompt(qb, kb, vb, lam_q, lam_k, subln_g, lam_init):
    b, s, w = qb.shape
    tq = _pick(s, (512, 256, 128, 64))
    hw = ATTN_HEADS_PER_STEP * HEAD_W
    chains = 2 * ATTN_HEADS_PER_STEP
    kern = functools.partial(_attn_prompt_kernel, tq=tq, lam_init=lam_init)
    small = lambda shape: pl.BlockSpec(shape, lambda bi, h, i: (0, 0))
    seq = lambda: pl.BlockSpec((None, s, hw), lambda bi, h, i: (bi, 0, h), pipeline_mode=pl.Buffered(1))
    return pl.pallas_call(
        kern,
        grid=(b, N_HEADS // ATTN_HEADS_PER_STEP, s // tq),
        in_specs=[small((2, HEAD_DIM)), small((2, HEAD_DIM)), small((1, HEAD_W)),
                  pl.BlockSpec((None, tq, hw), lambda bi, h, i: (bi, i, h)),
                  seq(), seq()],
        out_specs=pl.BlockSpec((None, tq, hw), lambda bi, h, i: (bi, i, h)),
        out_shape=jax.ShapeDtypeStruct((b, s, w), BF16),
        scratch_shapes=[pltpu.VMEM((chains, tq, LANES), F32), pltpu.VMEM((chains, tq, LANES), F32),
                        pltpu.VMEM((chains, tq, HEAD_W), F32),
                        pltpu.VMEM((ATTN_PREFETCH, tq, tq), F32)],
        compiler_params=_params("parallel", "parallel", "arbitrary"),
        name="attn_prompt",
    )(lam_q, lam_k, subln_g.reshape(1, HEAD_W), qb, kb, vb)


def _attn_sample_kernel(lq_ref, lk_ref, g_ref, q_ref, kn_ref, vn_ref, kp_ref, vp_ref, o_ref,
                        qbd_ref, bias_ref, m_ref, l_ref, acc_ref, *, lam_init, t):
    j = pl.program_id(1)
    c = (HEAD_DIM ** -0.5) * LOG2E
    rows = N_HEADS * 2 * t

    def attend(k_flat, v_flat):
        n = k_flat.shape[0]
        s = lax.dot_general(qbd_ref[...], k_flat, (((1,), (1,)), ((), ())),
                            preferred_element_type=F32)
        s = s + jnp.concatenate([bias_ref[...]] * (n // LANES), axis=1)
        m_prev = m_ref[...]
        m_new = jnp.maximum(m_prev, jnp.max(s, axis=1, keepdims=True))
        alpha = jnp.exp2((m_prev - m_new) * c)
        p = jnp.exp2((s - jnp.concatenate([m_new] * (n // LANES), axis=1)) * c)
        l_ref[...] = alpha * l_ref[...] + jnp.sum(p, axis=1, keepdims=True)
        acc_ref[...] = (acc_ref[...] * jnp.concatenate([alpha] * (HEAD_W // LANES), axis=1)
                        + jnp.dot(p.astype(BF16), v_flat, preferred_element_type=F32))
        m_ref[...] = m_new

    @pl.when(j == 0)
    def _():
        qbd_ref[...] = jnp.zeros(qbd_ref.shape, BF16)
        for h in range(N_HEADS):
            for comp in range(2):
                r0 = (h * 2 + comp) * t
                cols = slice(comp * HEAD_DIM, (comp + 1) * HEAD_DIM)
                qbd_ref[r0:r0 + t, cols] = q_ref[:, h * HEAD_W + comp * HEAD_DIM:
                                                 h * HEAD_W + (comp + 1) * HEAD_DIM]
        row_head = lax.broadcasted_iota(jnp.int32, (rows, LANES), 0) // (2 * t)
        key_head = lax.broadcasted_iota(jnp.int32, (rows, LANES), 1) % N_HEADS
        bias_ref[...] = jnp.where(row_head == key_head, 0.0, NEG_BIG)
        m_ref[...] = jnp.full(m_ref.shape, NEG_BIG, F32)
        l_ref[...] = jnp.zeros(l_ref.shape, F32)
        acc_ref[...] = jnp.zeros(acc_ref.shape, F32)
        attend(kn_ref[...].reshape(t * N_HEADS, HEAD_W).astype(BF16),
               vn_ref[...].reshape(t * N_HEADS, HEAD_W).astype(BF16))

    pb = kp_ref.shape[0]
    attend(kp_ref[...].reshape(pb * N_HEADS, HEAD_W).astype(BF16),
           vp_ref[...].reshape(pb * N_HEADS, HEAD_W).astype(BF16))

    @pl.when(j == pl.num_programs(1) - 1)
    def _():
        lam = _lambda_value(lq_ref, lk_ref, lam_init)
        rep = HEAD_W // LANES
        o = acc_ref[...] / jnp.concatenate([l_ref[...]] * rep, axis=1)
        for h in range(N_HEADS):
            r0 = h * 2 * t
            o_ref[:, h * HEAD_W:(h + 1) * HEAD_W] = _finish_heads(
                o[r0:r0 + t], o[r0 + t:r0 + 2 * t], lam, g_ref, lam_init).astype(o_ref.dtype)


def _attn_sample(qb, k_new, v_new, k_cache, v_cache, layer_idx, lam_q, lam_k, subln_g, lam_init):
    b, t, w = qb.shape
    p = k_cache.shape[2]
    pb = _pick(p, (512, 256, 128, 64, 32, 16))
    rows = N_HEADS * 2 * t
    kern = functools.partial(_attn_sample_kernel, lam_init=lam_init, t=t)
    small = lambda shape: pl.BlockSpec(shape, lambda bi, j: (0, 0))
    new = pl.BlockSpec((None, t, N_HEADS, HEAD_W), lambda bi, j: (bi, 0, 0, 0))
    past = pl.BlockSpec((None, None, pb, N_HEADS, HEAD_W), lambda bi, j: (layer_idx, bi, j, 0, 0))
    return pl.pallas_call(
        kern,
        grid=(b, p // pb),
        in_specs=[small((2, HEAD_DIM)), small((2, HEAD_DIM)), small((1, HEAD_W)),
                  pl.BlockSpec((None, t, w), lambda bi, j: (bi, 0, 0)),
                  new, new, past, past],
        out_specs=pl.BlockSpec((None, t, w), lambda bi, j: (bi, 0, 0)),
        out_shape=jax.ShapeDtypeStruct((b, t, w), BF16),
        scratch_shapes=[pltpu.VMEM((rows, HEAD_W), BF16), pltpu.VMEM((rows, LANES), F32),
                        pltpu.VMEM((rows, LANES), F32), pltpu.VMEM((rows, LANES), F32),
                        pltpu.VMEM((rows, HEAD_W), F32)],
        compiler_params=_params("parallel", "arbitrary"),
        name="attn_sample",
    )(lam_q, lam_k, subln_g.reshape(1, HEAD_W), qb, k_new, v_new, k_cache, v_cache)


def _conv_kernel(*refs, tm, has_halo):
    if has_halo:
        (past_ref, halo_ref, cur_ref, w_ref, b_ref, g_ref, beta_ref, o_ref,
         win_ref, sh_ref, y_ref) = refs
    else:
        past_ref, cur_ref, w_ref, b_ref, g_ref, beta_ref, o_ref, win_ref, sh_ref, y_ref = refs
        halo_ref = None
    i = pl.program_id(1)
    if has_halo:
        @pl.when(i == 0)
        def _():
            win_ref[0:CONV_HALO, :] = past_ref[...]

        @pl.when(i > 0)
        def _():
            win_ref[0:CONV_HALO, :] = halo_ref[...]
    else:
        win_ref[0:CONV_HALO, :] = past_ref[...]
    win_ref[CONV_HALO:, :] = cur_ref[...]

    width = cur_ref.shape[1]
    rc = _pick(tm, (32, 16, 8))
    lead = CONV_HALO - (CONV_W - 1)
    sh_rows = sh_ref.shape[1]

    def col_body(cj, carry):
        c0 = pl.multiple_of(cj * LANES, LANES)
        wk = [jnp.broadcast_to(w_ref[pl.ds(k, 1), pl.ds(c0, LANES)], (rc, LANES))
              for k in range(CONV_W)]
        bias = jnp.broadcast_to(b_ref[:, pl.ds(c0, LANES)], (rc, LANES))
        for s in range(1, SUBLANES):
            sh_ref[s - 1] = win_ref[pl.ds(s, sh_rows), pl.ds(c0, LANES)]
        for r in range(tm // rc):
            acc = bias
            for k in range(CONV_W):
                q, s = divmod(lead + k, SUBLANES)
                row = r * rc + q * SUBLANES
                if s == 0:
                    tap = win_ref[pl.ds(row, rc), pl.ds(c0, LANES)]
                else:
                    tap = sh_ref[s - 1, pl.ds(row, rc), :]
                acc = acc + tap * wk[k]
            y_ref[pl.ds(r * rc, rc), pl.ds(c0, LANES)] = acc
        return carry

    lax.fori_loop(0, width // LANES, col_body, 0)
    y = _layer_norm(y_ref[...], g_ref[...], beta_ref[...])
    o_ref[...] = (y * _sigmoid(y)).astype(o_ref.dtype)


def _conv_branch(c, past32, w_dw32, b_dw, cln_g, cln_b):
    b, t, ch = c.shape
    tm = _pick(t, (256, 128, 64, 32, 16, 8))
    has_halo = t > tm
    kern = functools.partial(_conv_kernel, tm=tm, has_halo=has_halo)
    vec = lambda: pl.BlockSpec((1, ch), lambda bi, i: (0, 0))
    hb = tm // CONV_HALO
    in_specs = [pl.BlockSpec((None, CONV_HALO, ch), lambda bi, i: (bi, 0, 0))]
    args = [past32]
    if has_halo:
        in_specs.append(pl.BlockSpec((None, CONV_HALO, ch),
                                     lambda bi, i: (bi, jnp.maximum(i * hb - 1, 0), 0)))
        args.append(c)
    in_specs += [pl.BlockSpec((None, tm, ch), lambda bi, i: (bi, i, 0)),
                 pl.BlockSpec((CONV_HALO, ch), lambda bi, i: (0, 0)),
                 vec(), vec(), vec()]
    args += [c, w_dw32, b_dw.reshape(1, ch), cln_g.reshape(1, ch), cln_b.reshape(1, ch)]
    return pl.pallas_call(
        kern,
        grid=(b, t // tm),
        in_specs=in_specs,
        out_specs=pl.BlockSpec((None, tm, ch), lambda bi, i: (bi, i, 0)),
        out_shape=jax.ShapeDtypeStruct((b, t, ch), BF16),
        scratch_shapes=[pltpu.VMEM((tm + CONV_HALO, ch), F32),
                        pltpu.VMEM((SUBLANES - 1, tm + CONV_HALO - SUBLANES, LANES), F32),
                        pltpu.VMEM((tm, ch), F32)],
        compiler_params=_params("parallel", "parallel"),
        name="conv_branch",
    )(*args)


def _mix_kernel(on_ref, ca_ref, ga_ref, gc_ref, wa_ref, wc_ref, o_ref):
    att = jnp.dot(on_ref[...], wa_ref[...], preferred_element_type=F32)
    cv = jnp.dot(ca_ref[...], wc_ref[...], preferred_element_type=F32)
    o_ref[...] = (ga_ref[...] * att + gc_ref[...] * cv).astype(o_ref.dtype)


def _branch_mix(o_n, cact, gates, w_att, w_conv):
    n, d = o_n.shape
    dm = w_att.shape[1]
    tm = _pick(n, (512, 256, 128, 64, 32, 16, 8))
    tn = _pick(dm, (1024, 512, 256, 128))
    nj = dm // tn
    return pl.pallas_call(
        _mix_kernel,
        grid=(nj, n // tm),
        in_specs=[pl.BlockSpec((tm, d), lambda j, i: (i, 0)),
                  pl.BlockSpec((tm, cact.shape[1]), lambda j, i: (i, 0)),
                  pl.BlockSpec((tm, tn), lambda j, i: (i, j)),
                  pl.BlockSpec((tm, tn), lambda j, i: (i, nj + j)),
                  pl.BlockSpec((d, tn), lambda j, i: (0, j)),
                  pl.BlockSpec((cact.shape[1], tn), lambda j, i: (0, j))],
        out_specs=pl.BlockSpec((tm, tn), lambda j, i: (i, j)),
        out_shape=jax.ShapeDtypeStruct((n, dm), BF16),
        compiler_params=_params("parallel", "parallel"),
        name="branch_mix",
    )(o_n, cact, gates, gates, w_att, w_conv)


def _out_ln_kernel(mix_ref, x_ref, w_ref, g_ref, b_ref, o_ref, ob_ref, *, alpha):
    y = alpha * x_ref[...] + jnp.dot(mix_ref[...], w_ref[...], preferred_element_type=F32)
    y = _layer_norm(y, g_ref[...], b_ref[...])
    o_ref[...] = y
    ob_ref[...] = y.astype(BF16)


def _out_ln(mix, x, w_o, g, b, alpha):
    n, d = x.shape
    tm = _pick(n, (512, 256, 128, 64, 32, 16, 8))
    vec = lambda: pl.BlockSpec((1, d), lambda i: (0, 0))
    return pl.pallas_call(
        functools.partial(_out_ln_kernel, alpha=alpha),
        grid=(n // tm,),
        in_specs=[pl.BlockSpec((tm, d), lambda i: (i, 0)),
                  pl.BlockSpec((tm, d), lambda i: (i, 0)),
                  _resident((d, d), lambda i: (0, 0)),
                  vec(), vec()],
        out_specs=[pl.BlockSpec((tm, d), lambda i: (i, 0)),
                   pl.BlockSpec((tm, d), lambda i: (i, 0))],
        out_shape=[jax.ShapeDtypeStruct((n, d), F32), jax.ShapeDtypeStruct((n, d), BF16)],
        compiler_params=_params("parallel"),
        name="out_ln1",
    )(mix, x, w_o, g.reshape(1, d), b.reshape(1, d))


def _peer_candidate_plan():
    return [PEER_TOPK // (a + 1) for a in range(PEER_TOPK)]


def _top_rows(s, count):
    rows, t = s.shape
    ridx = lax.broadcasted_iota(jnp.int32, (count, t), 0)
    out = jnp.zeros((count, t), F32)
    cur = s
    for r in range(count):
        m = jnp.max(cur, axis=0, keepdims=True)
        out = jnp.where(ridx == r, m, out)
        if r + 1 < count:
            cur = jnp.where(cur == m, -jnp.inf, cur)
    return out


def _sort_network(n):
    pairs = []
    p = 1
    while p < n:
        k = p
        while k >= 1:
            for j in range(k % p, n - k, 2 * k):
                for i in range(min(k, n - j - k)):
                    if (i + j) // (2 * p) == (i + j + k) // (2 * p):
                        pairs.append((i + j, i + j + k))
            k //= 2
        p *= 2
    return pairs


def _bitonic_network(n):
    pairs = []
    d = n // 2
    while d >= 1:
        for i in range(n):
            if (i // d) % 2 == 0:
                pairs.append((i, i + d))
        d //= 2
    return pairs


def _exchange(vals, pairs):
    for a, b in pairs:
        hi, lo = jnp.maximum(vals[a], vals[b]), jnp.minimum(vals[a], vals[b])
        vals[a], vals[b] = hi, lo
    return vals


def _top16_of_128(s):
    n = PEER_TOPK
    vals = _exchange([s[SUBLANES * k:SUBLANES * (k + 1), :] for k in range(n)], _sort_network(n))
    for shift in (4, 2, 1):
        other = [pltpu.roll(v, shift, 0) for v in vals]
        vals = [jnp.maximum(vals[k], other[n - 1 - k]) for k in range(n)]
        vals = _exchange(vals, _bitonic_network(n))
    row = lax.broadcasted_iota(jnp.int32, vals[0].shape, 0)
    groups = []
    for g in range(n // SUBLANES):
        acc = vals[g * SUBLANES]
        for k in range(1, SUBLANES):
            acc = jnp.where(row == k, vals[g * SUBLANES + k], acc)
        groups.append(acc)
    return jnp.concatenate(groups, axis=0), vals


def _peer_route_kernel(x_ref, wq_ref, keys_ref, lt_ref, rb_ref, e1_ref, e2_ref):
    q = jnp.dot(x_ref[...], wq_ref[...], preferred_element_type=F32).astype(BF16)
    tn = q.shape[0]
    limits = _peer_candidate_plan()
    row8 = lax.broadcasted_iota(jnp.int32, (8, tn), 0)
    for h in range(PEER_HEADS):
        st = []
        for half in range(2):
            col = (h * 2 + half) * PEER_NKEYS
            s = lax.dot_general(keys_ref[h, half], q[:, col:col + PEER_NKEYS],
                                (((1,), (1,)), ((), ())), preferred_element_type=F32)
            st.append(s * LOG2E)
        v1, v1_rows = _top16_of_128(st[0])
        v2, v2_rows = _top16_of_128(st[1])
        groups = [v1[0:1, :] + v2[0:8, :], v1[0:1, :] + v2[8:16, :]]
        for a in range(1, 8):
            cand = v1[a:a + 1, :] + v2[0:8, :]
            if limits[a] < 8:
                cand = jnp.where(row8 < limits[a], cand, -jnp.inf)
            groups.append(cand)
        groups.append(v1[8:16, :] + v2[0:1, :])
        cand = jnp.concatenate(groups, axis=0)
        top = _top_rows(cand, PEER_TOPK)
        z = jnp.sum(jnp.exp2(top - top[0:1, :]), axis=0, keepdims=True)
        tau = top[PEER_TOPK - 1:PEER_TOPK, :]
        count_by_rank = jnp.zeros((PEER_TOPK, tn), F32)
        for b in range(PEER_TOPK):
            count_by_rank = count_by_rank + jnp.where(v1 + v2[b:b + 1, :] >= tau, 1.0, 0.0)
        count_rows = [jnp.broadcast_to(count_by_rank[a:a + 1, :], (SUBLANES, tn))
                      for a in range(PEER_TOPK)]
        counts, ranks = [], []
        for k in range(PEER_NKEYS // SUBLANES):
            rows = slice(k * SUBLANES, (k + 1) * SUBLANES)
            s1k, s2k = st[0][rows, :], st[1][rows, :]
            count = jnp.zeros((SUBLANES, tn), F32)
            rank = jnp.full((SUBLANES, tn), float(PEER_TOPK), F32)
            for a in range(PEER_TOPK):
                count = jnp.where(s1k == v1_rows[a], count_rows[a], count)
                rank = jnp.where(s2k == v2_rows[a], float(a), rank)
            counts.append(count)
            ranks.append(rank)
        lt_ref[h] = jnp.concatenate(counts, axis=0)
        rb_ref[h] = jnp.concatenate(ranks, axis=0).astype(BF16)
        e1_ref[h] = jnp.exp2(st[0] - v1[0:1, :])
        e2_ref[h] = (jnp.exp2(st[1] - v2[0:1, :]) / z).astype(BF16)


def _peer_route(xb, wq, keys):
    n, d = xb.shape
    tn = _pick(n, (256, 128))
    qw = wq.shape[1]
    tab = lambda: pl.BlockSpec((PEER_HEADS, PEER_NKEYS, tn), lambda i: (0, 0, i))
    shape = lambda dt: jax.ShapeDtypeStruct((PEER_HEADS, PEER_NKEYS, n), dt)
    return pl.pallas_call(
        _peer_route_kernel,
        grid=(n // tn,),
        in_specs=[pl.BlockSpec((tn, d), lambda i: (i, 0)),
                  _resident((d, qw), lambda i: (0, 0)),
                  _resident(keys.shape, lambda i: (0, 0, 0, 0))],
        out_specs=[tab(), tab(), tab(), tab()],
        out_shape=[shape(F32), shape(BF16), shape(F32), shape(BF16)],
        compiler_params=_params("parallel"),
        name="peer_route",
    )(xb, wq, keys)


def _gelu_tanh(x):
    c = -2.0 * math.sqrt(2.0 / math.pi) * LOG2E
    w = x * ((x * x) * (0.044715 * c) + c)
    return x / (1.0 + jnp.exp2(w))


def _rows_bf16(row):
    packed = jnp.broadcast_to(row, (2 * SUBLANES, LANES)).astype(BF16)
    return jnp.concatenate([packed] * (PEER_NKEYS // (2 * SUBLANES)), axis=0)


PEER_STEP = SUBLANES * PEER_NKEYS
PEER_PARTS = (4, 4)


def _peer_mix_kernel(x_ref, lt_ref, rb_ref, e1_ref, e2_ref, u_ref, vt_ref, g_ref, b_ref,
                     o_ref, xb_ref, acc_ref, ht_ref, act_ref, *, alpha):
    eb = pl.program_id(1)
    tn = x_ref.shape[0]
    bounds = [sum(PEER_PARTS[:k]) for k in range(len(PEER_PARTS) + 1)]

    @pl.when(eb == 0)
    def _():
        xb_ref[...] = x_ref[...].T.astype(BF16)
        acc_ref[...] = jnp.zeros(acc_ref.shape, F32)

    for lo, hi in zip(bounds[:-1], bounds[1:]):
        es = slice(lo * PEER_NKEYS, hi * PEER_NKEYS)
        ht_ref[es, :] = jnp.dot(u_ref[es, :], xb_ref[...], preferred_element_type=F32)

    row8 = pl.multiple_of(eb * 8, 8)
    for lo, hi in zip(bounds[:-1], bounds[1:]):
        es = slice(lo * PEER_NKEYS, hi * PEER_NKEYS)
        for tc in range(tn // LANES):
            cs = slice(tc * LANES, (tc + 1) * LANES)
            c_grp = [lt_ref[h, pl.ds(row8, 8), cs] for h in range(PEER_HEADS)]
            e_grp = [e1_ref[h, pl.ds(row8, 8), cs] for h in range(PEER_HEADS)]
            for r in range(lo, hi):
                gsum = jnp.zeros((PEER_NKEYS, LANES), BF16)
                for h in range(PEER_HEADS):
                    chosen = rb_ref[h, :, cs] < _rows_bf16(c_grp[h][r:r + 1])
                    gate = jnp.where(chosen, e2_ref[h, :, cs], 0.0) * _rows_bf16(e_grp[h][r:r + 1])
                    gsum = gsum + gate
                rs = slice(r * PEER_NKEYS, (r + 1) * PEER_NKEYS)
                act_ref[rs, cs] = _gelu_tanh(ht_ref[rs, cs]).astype(BF16) * gsum
        acc_ref[...] += jnp.dot(vt_ref[:, es], act_ref[es, :], preferred_element_type=F32)

    @pl.when(eb == pl.num_programs(1) - 1)
    def _():
        y = alpha * x_ref[...] + acc_ref[...].T
        o_ref[...] = _layer_norm(y, g_ref[...], b_ref[...])


def _peer_mix(x, tabs, u, vt, g, b, alpha):
    n, d = x.shape
    e = u.shape[0]
    assert e == PEER_NKEYS * PEER_NKEYS and e % PEER_STEP == 0
    count1, rank2, e1, e2 = tabs
    tn = _pick(n, (512, 256, 128))
    once = lambda shape, imap: pl.BlockSpec(shape, imap, pipeline_mode=pl.Buffered(1))
    tab = lambda: once((PEER_HEADS, PEER_NKEYS, tn), lambda t, j: (0, 0, t))
    vec = lambda: pl.BlockSpec((1, d), lambda t, j: (0, 0))
    kern = functools.partial(_peer_mix_kernel, alpha=alpha)
    return pl.pallas_call(
        kern,
        grid=(n // tn, e // PEER_STEP),
        in_specs=[once((tn, d), lambda t, j: (t, 0)),
                  tab(), tab(), tab(), tab(),
                  pl.BlockSpec((PEER_STEP, d), lambda t, j: (j, 0)),
                  pl.BlockSpec((d, PEER_STEP), lambda t, j: (0, j)),
                  vec(), vec()],
        out_specs=pl.BlockSpec((tn, d), lambda t, j: (t, 0)),
        out_shape=jax.ShapeDtypeStruct((n, d), F32),
        scratch_shapes=[pltpu.VMEM((d, tn), BF16), pltpu.VMEM((d, tn), F32),
                        pltpu.VMEM((PEER_STEP, tn), F32), pltpu.VMEM((PEER_STEP, tn), BF16)],
        compiler_params=_params("parallel", "arbitrary"),
        name="peer_mix",
    )(x, count1, rank2, e1, e2, u, vt, g.reshape(1, d), b.reshape(1, d))


def _ple_kernel(xr_ref, xc_ref, pe_ref, wg_ref, wp_ref, o_ref):
    gate = _sigmoid(jnp.dot(xr_ref[...].astype(BF16), wg_ref[...], preferred_element_type=F32))
    emb = jnp.dot(pe_ref[...].astype(BF16), wp_ref[...], preferred_element_type=F32)
    o_ref[...] = xc_ref[...] + gate * emb


def _ple(x, pe, w_gate, w_ple):
    n, d = x.shape
    pd = pe.shape[1]
    tm = _pick(n, (512, 256, 128, 64, 32, 16, 8))
    tn = _pick(d, (1024, 512, 256, 128))
    return pl.pallas_call(
        _ple_kernel,
        grid=(d // tn, n // tm),
        in_specs=[pl.BlockSpec((tm, d), lambda j, i: (i, 0)),
                  pl.BlockSpec((tm, tn), lambda j, i: (i, j)),
                  pl.BlockSpec((tm, pd), lambda j, i: (i, 0)),
                  pl.BlockSpec((d, tn), lambda j, i: (0, j)),
                  pl.BlockSpec((pd, tn), lambda j, i: (0, j))],
        out_specs=pl.BlockSpec((tm, tn), lambda j, i: (i, j)),
        out_shape=jax.ShapeDtypeStruct((n, d), F32),
        compiler_params=_params("parallel", "parallel"),
        name="ple",
    )(x, x, pe, w_gate, w_ple)


def _rope_tables(pos, rows_per_block_hint):
    inv = ROPE_THETA ** (-jnp.arange(0, ROT_DIM, 2, dtype=F32) / ROT_DIM)
    ang = pos.astype(F32)[:, None] * inv[None, :]
    cos, sin = jnp.cos(ang), jnp.sin(ang)
    t = pos.shape[0]
    half = ROT_DIM // 2
    ones = jnp.ones((t, HEAD_DIM - ROT_DIM), F32)
    zeros_h = jnp.zeros((t, half), F32)
    zeros_r = jnp.zeros((t, HEAD_DIM - ROT_DIM), F32)
    c = jnp.concatenate([cos, cos, ones], axis=1)
    sa = jnp.concatenate([-sin, zeros_h, zeros_r], axis=1)
    sb = jnp.concatenate([zeros_h, sin, zeros_r], axis=1)
    if t < rows_per_block_hint:
        reps = rows_per_block_hint // t
        c, sa, sb = (jnp.tile(a, (reps, 1)) for a in (c, sa, sb))
    return c, sa, sb


def _layer(x, pe, pos, k_past, v_past, conv_past, layer_idx, depth, wts):
    (w_in, lam_q, lam_k, subln_g, w_att_out, w_dw, b_dw, cln_g, cln_b, w_conv_out, w_o,
     ln1_g, ln1_b, peer_wq, peer_keys, peer_u, peer_v, ln2_g, ln2_b, w_ple, w_ple_gate) = wts
    b, t, d = x.shape
    n = b * t
    att_w = N_HEADS * HEAD_W
    conv_dim = w_dw.shape[1]
    alpha = (2 * depth) ** 0.25
    lam_init = 0.8 - 0.6 * math.exp(-0.3 * layer_idx)

    x2d = x.reshape(n, d)
    xb = x2d.astype(BF16)
    tm, _ = _proj_tiles(n, att_w)
    tabs = _rope_tables(pos, tm)

    (qb,) = _rope_proj(xb, w_in, 0, att_w, tabs, (BF16,))
    c = _glu_proj(xb, w_in, 3 * att_w, conv_dim)
    gates = _sigmoid_proj(xb, w_in, 3 * att_w + 2 * conv_dim, 2 * d)

    shp = (b, t, att_w)
    heads = (b, t, N_HEADS, HEAD_W)
    if k_past is None:
        k_f32, kb = _rope_proj(xb, w_in, att_w, att_w, tabs, (F32, BF16))
        v_f32, vb = _plain_proj(xb, w_in, 2 * att_w, att_w, (F32, BF16))
        o_n = _attn_prompt(qb.reshape(shp), kb.reshape(shp), vb.reshape(shp),
                           lam_q, lam_k, subln_g, lam_init)
    else:
        (k_f32,) = _rope_proj(xb, w_in, att_w, att_w, tabs, (F32,))
        (v_f32,) = _plain_proj(xb, w_in, 2 * att_w, att_w, (F32,))
        o_n = _attn_sample(qb.reshape(shp), k_f32.reshape(heads), v_f32.reshape(heads),
                           k_past, v_past, layer_idx, lam_q, lam_k, subln_g, lam_init)

    c3 = c.reshape(b, t, conv_dim)
    past32 = jnp.pad(conv_past, ((0, 0), (CONV_HALO - (CONV_W - 1), 0), (0, 0)))
    w_dw32 = jnp.pad(w_dw, ((0, CONV_HALO - CONV_W), (0, 0)))
    cact = _conv_branch(c3, past32, w_dw32, b_dw, cln_g, cln_b)
    conv_state = jnp.concatenate([conv_past, c3], axis=1)[:, -(CONV_W - 1):]

    mix = _branch_mix(o_n.reshape(n, att_w), cact.reshape(n, conv_dim), gates, w_att_out, w_conv_out)
    x1, x1b = _out_ln(mix, x2d, w_o, ln1_g, ln1_b, alpha)

    route = _peer_route(x1b, peer_wq, peer_keys)
    x2 = _peer_mix(x1, route, peer_u, peer_v, ln2_g, ln2_b, alpha)

    y = _ple(x2, pe.reshape(n, pe.shape[-1]), w_ple_gate, w_ple)
    return (y.reshape(b, t, d), k_f32.reshape(b, t, N_HEADS, HEAD_W),
            v_f32.reshape(b, t, N_HEADS, HEAD_W), conv_state)


def kernel(x_prompt, x_sample, cache_k, cache_v, state_conv, p_prompt, p_sample, w_in, lam_q, lam_k, subln_g, w_att_out, w_dw, b_dw, cln_g, cln_b, w_conv_out, w_o, ln1_g, ln1_b, peer_wq, peer_keys, peer_u, peer_v, ln2_g, ln2_b, w_ple, w_ple_gate):
    depth = w_in.shape[0]
    xp, xs = x_prompt, x_sample
    pos_p = jnp.arange(x_prompt.shape[1])
    pos_s = cache_k.shape[2] + jnp.arange(x_sample.shape[1])
    mats = (w_in, w_att_out, w_conv_out, w_o, peer_wq, peer_keys, peer_u, peer_v, w_ple, w_ple_gate)
    w_in_b, w_att_b, w_conv_b, w_o_b, wq_b, keys_b, u_b, v_b, w_ple_b, w_pg_b = (
        m.astype(BF16) for m in mats)
    v_b = jnp.swapaxes(v_b, 1, 2)
    outs = [[] for _ in range(6)]
    for l in range(depth):
        wts = (w_in_b[l], lam_q[l], lam_k[l], subln_g[l], w_att_b[l], w_dw[l], b_dw[l],
               cln_g[l], cln_b[l], w_conv_b[l], w_o_b[l], ln1_g[l], ln1_b[l],
               wq_b[l], keys_b[l], u_b[l], v_b[l], ln2_g[l], ln2_b[l], w_ple_b[l], w_pg_b[l])
        zero_conv = jnp.zeros((xp.shape[0], CONV_W - 1, w_dw.shape[2]), xp.dtype)
        xp, kp, vp, cp = _layer(xp, p_prompt[l], pos_p, None, None, zero_conv, l, depth, wts)
        xs, ks, vs, cs = _layer(xs, p_sample[l], pos_s, cache_k, cache_v, state_conv[l],
                                l, depth, wts)
        for lst, val in zip(outs, (kp, vp, cp, ks, vs, cs)):
            lst.append(val)
    return (xp, xs) + tuple(jnp.stack(o) for o in outs)
```
